```python
import jax, jax.numpy as jnp
from jax import lax
import numpy as np

D_MODEL = 1024
BATCH = 4
SEQ = 8192
DEPTH = 1

GRID_W = 64
CTX_LEN = 256
HEAD_DIM = 64
N_Q_HEADS = 8
N_KV_HEADS = 2
Q_PER_KV = N_Q_HEADS // N_KV_HEADS
ATTN_WIDTH = N_Q_HEADS * HEAD_DIM
GM_GROUPS = 8
GM_GROUP_DIM = 64
GM_WIDTH = GM_GROUPS * GM_GROUP_DIM
CHUNK = 128
Q_BLOCK = 128
D_FF = 4 * D_MODEL
ROPE_THETA = 10000.0
ROT_AXIS_DIM = HEAD_DIM // 2
EPS = 1e-6
N_MOD = 6

K_W = N_KV_HEADS * HEAD_DIM
V_W = N_KV_HEADS * HEAD_DIM
KV_COLS = K_W + V_W
Q_W = ATTN_WIDTH
U_W = GM_WIDTH
VG_W = GM_WIDTH
GA_W = D_MODEL
GB_W = D_MODEL
D_IN = KV_COLS + Q_W + U_W + VG_W + GA_W + GB_W
REST_SPLITS = tuple(int(s) for s in np.cumsum([Q_W, U_W, VG_W, GA_W]))

kernel_name = "hybrid_gqa_gmlp_dit_block"


def rmsnorm(x, g):
    xf = x.astype(jnp.float32)
    y = xf * lax.rsqrt(jnp.mean(xf * xf, axis=-1, keepdims=True) + EPS)
    return (y * g.astype(jnp.float32)).astype(x.dtype)


def modulate(x, g, shift, scale):
    return rmsnorm(x, g) * (1 + scale) + shift


def axial_rope_tables(n_tokens, dtype):
    rows = n_tokens // GRID_W
    row = jnp.repeat(jnp.arange(rows, dtype=jnp.float32), GRID_W)
    col = jnp.tile(jnp.arange(GRID_W, dtype=jnp.float32), rows)
    inv = ROPE_THETA ** (-jnp.arange(0, ROT_AXIS_DIM, 2, dtype=jnp.float32) / ROT_AXIS_DIM)
    ang = jnp.concatenate([row[:, None] * inv, col[:, None] * inv], axis=-1)
    return jnp.cos(ang).astype(dtype), jnp.sin(ang).astype(dtype)


def apply_rope(x, cos, sin):
    x1, x2 = x[..., :HEAD_DIM // 2], x[..., HEAD_DIM // 2:]
    return jnp.concatenate([x1 * cos - x2 * sin, x2 * cos + x1 * sin], axis=-1)


def q_heads(p):
    b, n, _ = p.shape
    return p.reshape(b, n, N_KV_HEADS, Q_PER_KV, HEAD_DIM).transpose(0, 2, 3, 1, 4)


def kv_heads(p):
    b, n, _ = p.shape
    return p.reshape(b, n, N_KV_HEADS, HEAD_DIM).transpose(0, 2, 1, 3)


def merge_heads(o):
    b, kv, g, n, d = o.shape
    return o.transpose(0, 3, 1, 2, 4).reshape(b, n, kv * g * d)


def attend(q, k, v):
    s = jnp.einsum('bkgqd,bknd->bkgqn', q, k, preferred_element_type=jnp.float32)
    p = jax.nn.softmax(s, axis=-1).astype(v.dtype)
    return jnp.einsum('bkgqn,bknd->bkgqd', p, v)


def attend_blocked(q, k, v):
    b, kv, g, n, d = q.shape
    nb = n // Q_BLOCK
    qb = jnp.moveaxis(q.reshape(b, kv, g, nb, Q_BLOCK, d), 3, 0)
    ob = lax.map(lambda blk: attend(blk, k, v), qb)
    return jnp.moveaxis(ob, 0, 3).reshape(b, kv, g, n, d)


def gmlp_spatial(u, v, gm_norm_g, gm_ws, gm_bs):
    b, n, _ = u.shape
    nc = n // CHUNK
    vg = v.reshape(b, nc, CHUNK, GM_GROUPS, GM_GROUP_DIM)
    vg = rmsnorm(vg, gm_norm_g)
    s = jnp.einsum('gpq,bnqgc->bnpgc', gm_ws.astype(vg.dtype), vg) + gm_bs.T[:, :, None].astype(vg.dtype)
    return u * s.reshape(b, n, GM_WIDTH)


def branch_merge(attn_o, gm_o, ga_logit, gb_logit, w_br_attn, w_br_gm, w_out):
    y = jax.nn.sigmoid(ga_logit) * (attn_o @ w_br_attn) + jax.nn.sigmoid(gb_logit) * (gm_o @ w_br_gm)
    return y @ w_out


def sq_relu_mlp(h, w1, w2):
    return jnp.square(jax.nn.relu(h @ w1)) @ w2


def setup_inputs(seed: int = 0) -> dict:
    key = jax.random.key(seed)
    ks = jax.random.split(key, 20)
    f32 = jnp.float32
    nrm = lambda k, shape, s: jax.random.normal(k, shape, f32) * s
    return {
        "x": nrm(ks[0], (BATCH, SEQ, D_MODEL), 1.0),
        "c": nrm(ks[1], (BATCH, D_MODEL), 1.0),
        "ctx": nrm(ks[2], (BATCH, CTX_LEN, D_MODEL), 1.0),
        "c_ctx": nrm(ks[3], (D_MODEL,), 1.0),
        "w_mod": nrm(ks[4], (DEPTH, D_MODEL, N_MOD * D_MODEL), 0.02),
        "b_mod": nrm(ks[5], (DEPTH, N_MOD * D_MODEL), 0.02),
        "norm1_g": 1.0 + nrm(ks[6], (DEPTH, D_MODEL), 0.02),
        "norm2_g": 1.0 + nrm(ks[7], (DEPTH, D_MODEL), 0.02),
        "w_in": nrm(ks[8], (DEPTH, D_MODEL, D_IN), D_MODEL ** -0.5),
        "q_norm_g": 1.0 + nrm(ks[9], (DEPTH, HEAD_DIM), 0.02),
        "k_norm_g": 1.0 + nrm(ks[10], (DEPTH, HEAD_DIM), 0.02),
        "gm_norm_g": 1.0 + nrm(ks[11], (DEPTH, GM_GROUPS, GM_GROUP_DIM), 0.02),
        "gm_ws": nrm(ks[12], (DEPTH, GM_GROUPS, CHUNK, CHUNK), CHUNK ** -0.5),
        "gm_bs": 1.0 + nrm(ks[13], (DEPTH, GM_GROUPS, CHUNK), 0.02),
        "w_br_attn": nrm(ks[14], (DEPTH, ATTN_WIDTH, D_MODEL), ATTN_WIDTH ** -0.5),
        "w_br_gm": nrm(ks[15], (DEPTH, GM_WIDTH, D_MODEL), GM_WIDTH ** -0.5),
        "w_out": nrm(ks[16], (DEPTH, D_MODEL, D_MODEL), D_MODEL ** -0.5),
        "w_ff1": nrm(ks[17], (DEPTH, D_MODEL, D_FF), D_MODEL ** -0.5),
        "w_ff2": nrm(ks[18], (DEPTH, D_FF, D_MODEL), D_FF ** -0.5),
    }


def reference(x, c, ctx, c_ctx, w_mod, b_mod, norm1_g, norm2_g, w_in, q_norm_g, k_norm_g,
              gm_norm_g, gm_ws, gm_bs, w_br_attn, w_br_gm, w_out, w_ff1, w_ff2):
    n_tok = x.shape[1]
    cos, sin = axial_rope_tables(n_tok, x.dtype)
    q_scale = HEAD_DIM ** -0.5
    ctx_s = ctx
    for l in range(DEPTH):
        more_layers = l + 1 < DEPTH
        mod_x = jax.nn.silu(c) @ w_mod[l] + b_mod[l]
        sh1, sc1, g1, sh2, sc2, g2 = jnp.split(mod_x[:, None, :], N_MOD, axis=-1)
        mod_c = jax.nn.silu(c_ctx) @ w_mod[l] + b_mod[l]
        sh1c, sc1c, g1c, sh2c, sc2c, g2c = jnp.split(mod_c, N_MOD, axis=-1)

        h_c = modulate(ctx_s, norm1_g[l], sh1c, sc1c)
        n_cols = D_IN if more_layers else KV_COLS
        p_c = h_c @ w_in[l][:, :n_cols]
        k_c = rmsnorm(kv_heads(p_c[..., :K_W]), k_norm_g[l])
        v_c = kv_heads(p_c[..., K_W:KV_COLS])

        h_x = modulate(x, norm1_g[l], sh1, sc1)
        p_x = h_x @ w_in[l]
        k_x = apply_rope(rmsnorm(kv_heads(p_x[..., :K_W]), k_norm_g[l]), cos, sin)
        v_x = kv_heads(p_x[..., K_W:KV_COLS])
        q_x, u_x, vg_x, ga_x, gb_x = jnp.split(p_x[..., KV_COLS:], REST_SPLITS, axis=-1)
        q_x = apply_rope(rmsnorm(q_heads(q_x), q_norm_g[l]), cos, sin) * q_scale
        k_all = jnp.concatenate([k_c, k_x], axis=2)
        v_all = jnp.concatenate([v_c, v_x], axis=2)
        attn_x = merge_heads(attend_blocked(q_x, k_all, v_all))
        gm_x = gmlp_spatial(jax.nn.gelu(u_x), jax.nn.gelu(vg_x), gm_norm_g[l], gm_ws[l], gm_bs[l])
        x = x + g1 * branch_merge(attn_x, gm_x, ga_x, gb_x, w_br_attn[l], w_br_gm[l], w_out[l])

        h2 = modulate(x, norm2_g[l], sh2, sc2)
        x = x + g2 * sq_relu_mlp(h2, w_ff1[l], w_ff2[l])

        if more_layers:
            q_c, u_c, vg_c, ga_c, gb_c = jnp.split(p_c[..., KV_COLS:], REST_SPLITS, axis=-1)
            q_c = rmsnorm(q_heads(q_c), q_norm_g[l]) * q_scale
            attn_c = merge_heads(attend(q_c, k_c, v_c))
            gm_c = gmlp_spatial(jax.nn.gelu(u_c), jax.nn.gelu(vg_c), gm_norm_g[l], gm_ws[l], gm_bs[l])
            ctx_s = ctx_s + g1c * branch_merge(attn_c, gm_c, ga_c, gb_c, w_br_attn[l], w_br_gm[l], w_out[l])
            h2c = modulate(ctx_s, norm2_g[l], sh2c, sc2c)
            ctx_s = ctx_s + g2c * sq_relu_mlp(h2c, w_ff1[l], w_ff2[l])
    return x
```

```python
import functools
import math

import jax
import jax.numpy as jnp
from jax import lax
from jax.experimental import pallas as pl
from jax.experimental.pallas import tpu as pltpu

F32 = jnp.float32
BF16 = jnp.bfloat16

D_MODEL = 1024
GRID_W = 64
HEAD_DIM = 64
N_Q_HEADS = 8
N_KV_HEADS = 2
Q_PER_KV = N_Q_HEADS // N_KV_HEADS
ATTN_WIDTH = N_Q_HEADS * HEAD_DIM
GM_GROUPS = 8
GM_GROUP_DIM = 64
GM_WIDTH = GM_GROUPS * GM_GROUP_DIM
CHUNK = 128
D_FF = 4 * D_MODEL
ROPE_THETA = 10000.0
ROT_AXIS_DIM = HEAD_DIM // 2
EPS = 1e-6
N_MOD = 6

K_W = N_KV_HEADS * HEAD_DIM
KV_COLS = 2 * K_W
Q_OFF = KV_COLS
U_OFF = Q_OFF + ATTN_WIDTH
VG_OFF = U_OFF + GM_WIDTH
GA_OFF = VG_OFF + GM_WIDTH
GB_OFF = GA_OFF + D_MODEL
D_IN = GB_OFF + D_MODEL

V7X_VMEM_LIMIT_BYTES = 56 * 1024 * 1024

TOK_TILE = 256
Q_TILE = 256
KV_TILE = 512
ONES_ROWS = 16

Q_SCALE_LOG2E = (HEAD_DIM ** -0.5) * math.log2(math.e)


def _sigmoid(x):
    return 1.0 / (1.0 + jnp.exp(-x))


def _gelu_tanh(x):
    c = math.sqrt(2.0 / math.pi)
    return 0.5 * x * (1.0 + jnp.tanh(c * (x + 0.044715 * (x * x * x))))


def _modulate(x, g, shift, scale):
    ms = jnp.mean(x * x, axis=-1, keepdims=True)
    return (x * lax.rsqrt(ms + EPS) * g) * (1.0 + scale) + shift


def _const_spec(shape):
    return pl.BlockSpec(shape, lambda *_: (0,) * len(shape), pipeline_mode=pl.Buffered(1))


def _adaln_kernel(c_ref, w_ref, b_ref, o_ref):
    c = c_ref[...]
    a = (c * _sigmoid(c)).astype(BF16)
    o_ref[...] = jnp.dot(a, w_ref[...].astype(BF16), preferred_element_type=F32) + b_ref[...]


def _adaln(c_all, w_mod, b_mod):
    rows, d = c_all.shape
    n = w_mod.shape[1]
    tn = 1024
    return pl.pallas_call(
        _adaln_kernel,
        grid=(n // tn,),
        in_specs=[
            pl.BlockSpec((rows, d), lambda j: (0, 0)),
            pl.BlockSpec((d, tn), lambda j: (0, j)),
            pl.BlockSpec((1, tn), lambda j: (0, j)),
        ],
        out_specs=pl.BlockSpec((rows, tn), lambda j: (0, j)),
        out_shape=jax.ShapeDtypeStruct((rows, n), F32),
        compiler_params=pltpu.CompilerParams(
            dimension_semantics=("arbitrary",), vmem_limit_bytes=V7X_VMEM_LIMIT_BYTES),
        name="adaln",
    )(c_all, w_mod, b_mod.reshape(1, n))


def _head_norm_t(blk, g_col):
    ms = jnp.mean(blk * blk, axis=0, keepdims=True)
    return blk * lax.rsqrt(ms + EPS) * g_col


def _rope_t(y, cos_t, sin_t):
    y1, y2 = y[:ROT_AXIS_DIM], y[ROT_AXIS_DIM:]
    return jnp.concatenate([y1 * cos_t - y2 * sin_t, y2 * cos_t + y1 * sin_t], axis=0)


def _ctx_kv_kernel(x_ref, mod_ref, g1_ref, w_ref, kg_ref, k_ref, vt_ref):
    x = x_ref[0]
    mod = mod_ref[0]
    h = _modulate(x, g1_ref[...], mod[0:1], mod[1:2]).astype(BF16)
    p = jnp.dot(h, w_ref[...], preferred_element_type=F32)
    pt = p.T
    kg = kg_ref[...]
    kt = jnp.concatenate(
        [_head_norm_t(pt[HEAD_DIM * j:HEAD_DIM * (j + 1)], kg) for j in range(N_KV_HEADS)], axis=0)
    k_ref[0] = kt.T.astype(BF16)
    for j in range(N_KV_HEADS):
        vt_ref[0, j] = pt[K_W + HEAD_DIM * j:K_W + HEAD_DIM * (j + 1)].astype(BF16)


def _ctx_kv(ctx, mods3, norm1_g, w_kv, k_norm_col, mod_row):
    b, n, d = ctx.shape
    return pl.pallas_call(
        _ctx_kv_kernel,
        grid=(b,),
        in_specs=[
            pl.BlockSpec((1, n, d), lambda i: (i, 0, 0)),
            pl.BlockSpec((1, N_MOD, d), lambda i: (mod_row, 0, 0)),
            _const_spec((1, d)),
            _const_spec((d, KV_COLS)),
            _const_spec((HEAD_DIM, 1)),
        ],
        out_specs=[
            pl.BlockSpec((1, n, K_W), lambda i: (i, 0, 0)),
            pl.BlockSpec((1, N_KV_HEADS, HEAD_DIM, n), lambda i: (i, 0, 0, 0)),
        ],
        out_shape=[
            jax.ShapeDtypeStruct((b, n, K_W), BF16),
            jax.ShapeDtypeStruct((b, N_KV_HEADS, HEAD_DIM, n), BF16),
        ],
        compiler_params=pltpu.CompilerParams(
            dimension_semantics=("arbitrary",), vmem_limit_bytes=V7X_VMEM_LIMIT_BYTES),
        name="ctx_kv",
    )(ctx, mods3, norm1_g, w_kv, k_norm_col)


def _in_proj_kernel(x_ref, mod_ref, g1_ref, w_ref, kg_ref, qg_ref, gmg_ref, e_ref, cos_ref, sin_ref,
                    k_ref, vt_ref, qt_ref, u_ref, vg_ref, ga_ref, gb_ref):
    x = x_ref[0]
    mod = mod_ref[0]
    h = _modulate(x, g1_ref[...], mod[0:1], mod[1:2]).astype(BF16)
    p = jnp.dot(h, w_ref[...], preferred_element_type=F32)

    cos_t = cos_ref[...]
    sin_t = sin_ref[...]
    pt = p[:, :U_OFF].T
    kg = kg_ref[...]
    kt = jnp.concatenate(
        [_rope_t(_head_norm_t(pt[HEAD_DIM * j:HEAD_DIM * (j + 1)], kg), cos_t, sin_t)
         for j in range(N_KV_HEADS)], axis=0)
    k_ref[0] = kt.T.astype(BF16)
    for j in range(N_KV_HEADS):
        vt_ref[0, j] = pt[K_W + HEAD_DIM * j:K_W + HEAD_DIM * (j + 1)].astype(BF16)
    qg = qg_ref[...]
    for j in range(N_Q_HEADS):
        blk = pt[Q_OFF + HEAD_DIM * j:Q_OFF + HEAD_DIM * (j + 1)]
        qt_ref[0, j] = (_rope_t(_head_norm_t(blk, qg), cos_t, sin_t) * Q_SCALE_LOG2E).astype(BF16)

    u_ref[0] = _gelu_tanh(p[:, U_OFF:VG_OFF]).astype(BF16)
    vg = _gelu_tanh(p[:, VG_OFF:GA_OFF])
    ss = jnp.dot((vg * vg).astype(BF16), e_ref[...], preferred_element_type=F32) * (1.0 / GM_GROUP_DIM)
    vg_ref[0] = (vg * lax.rsqrt(ss + EPS) * gmg_ref[...]).astype(BF16)
    ga_ref[0] = _sigmoid(p[:, GA_OFF:GB_OFF]).astype(BF16)
    gb_ref[0] = _sigmoid(p[:, GB_OFF:D_IN]).astype(BF16)


def _in_proj(x, mods3, norm1_g, w_in, k_norm_col, q_norm_col, gm_norm_row, group_ones, cos_t, sin_t):
    b, n, d = x.shape
    tm = TOK_TILE
    tok = lambda w: pl.BlockSpec((1, tm, w), lambda i, t: (i, t, 0))
    return pl.pallas_call(
        _in_proj_kernel,
        grid=(b, n // tm),
        in_specs=[
            tok(d),
            pl.BlockSpec((1, N_MOD, d), lambda i, t: (i, 0, 0)),
            _const_spec((1, d)),
            _const_spec((d, D_IN)),
            _const_spec((HEAD_DIM, 1)),
            _const_spec((HEAD_DIM, 1)),
            _const_spec((1, GM_WIDTH)),
            _const_spec((GM_WIDTH, GM_WIDTH)),
            pl.BlockSpec((ROT_AXIS_DIM, tm), lambda i, t: (0, t)),
            pl.BlockSpec((ROT_AXIS_DIM, tm), lambda i, t: (0, t)),
        ],
        out_specs=[
            tok(K_W),
            pl.BlockSpec((1, N_KV_HEADS, HEAD_DIM, tm), lambda i, t: (i, 0, 0, t)),
            pl.BlockSpec((1, N_Q_HEADS, HEAD_DIM, tm), lambda i, t: (i, 0, 0, t)),
            tok(GM_WIDTH), tok(GM_WIDTH), tok(d), tok(d),
        ],
        out_shape=[
            jax.ShapeDtypeStruct((b, n, K_W), BF16),
            jax.ShapeDtypeStruct((b, N_KV_HEADS, HEAD_DIM, n), BF16),
            jax.ShapeDtypeStruct((b, N_Q_HEADS, HEAD_DIM, n), BF16),
            jax.ShapeDtypeStruct((b, n, GM_WIDTH), BF16),
            jax.ShapeDtypeStruct((b, n, GM_WIDTH), BF16),
            jax.ShapeDtypeStruct((b, n, d), BF16),
            jax.ShapeDtypeStruct((b, n, d), BF16),
        ],
        compiler_params=pltpu.CompilerParams(
            dimension_semantics=("arbitrary", "arbitrary"), vmem_limit_bytes=V7X_VMEM_LIMIT_BYTES),
        name="in_proj",
    )(x, mods3, norm1_g, w_in, k_norm_col, q_norm_col, gm_norm_row, group_ones, cos_t, sin_t)


def _attention_kernel(qt_ref, k_ref, kc_ref, vt_ref, vct_ref, o_ref, qz_ref, m_ref, acc_ref):
    kv_head = pl.program_id(1)
    tq = qt_ref.shape[3]
    n_keys = k_ref.shape[1]

    q_cat = jnp.concatenate([qt_ref[0, g] for g in range(Q_PER_KV)], axis=1)
    zeros = jnp.zeros_like(q_cat)
    qz_ref[...] = jnp.concatenate(
        [jnp.where(kv_head == 0, q_cat, zeros), jnp.where(kv_head == 1, q_cat, zeros)], axis=0)

    def with_ones(vt):
        return jnp.concatenate([vt, jnp.ones((ONES_ROWS, vt.shape[1]), vt.dtype)], axis=0)

    s = jnp.dot(kc_ref[0], qz_ref[...], preferred_element_type=F32)
    m0 = jnp.max(s, axis=0, keepdims=True)
    p = jnp.exp2(s - m0).astype(BF16)
    m_ref[...] = m0
    acc_ref[...] = jnp.dot(with_ones(vct_ref[0, 0]), p, preferred_element_type=F32)

    def step(i, carry):
        start = pl.multiple_of(i * KV_TILE, KV_TILE)
        k_t = k_ref[0, pl.ds(start, KV_TILE), :]
        v1t = with_ones(vt_ref[0, 0, :, pl.ds(start, KV_TILE)])
        s = jnp.dot(k_t, qz_ref[...], preferred_element_type=F32)
        m_old = m_ref[...]
        m_new = jnp.maximum(m_old, jnp.max(s, axis=0, keepdims=True))
        alpha = jnp.exp2(m_old - m_new)
        p = jnp.exp2(s - m_new).astype(BF16)
        acc_ref[...] = alpha * acc_ref[...] + jnp.dot(v1t, p, preferred_element_type=F32)
        m_ref[...] = m_new
        return carry

    lax.fori_loop(0, n_keys // KV_TILE, step, 0)

    acc = acc_ref[...]
    o_t = acc[:HEAD_DIM] / acc[HEAD_DIM:HEAD_DIM + 1]
    pairs = []
    for g in range(0, Q_PER_KV, 2):
        two = jnp.concatenate([o_t[:, g * tq:(g + 1) * tq], o_t[:, (g + 1) * tq:(g + 2) * tq]], axis=0)
        pairs.append(two.T)
    o_ref[0] = jnp.concatenate(pairs, axis=1).astype(o_ref.dtype)


def _attention(qt, k, kc, vt, vct):
    b, _, _, n = qt.shape
    n_ctx = kc.shape[1]
    tq = Q_TILE
    cols = Q_PER_KV * tq
    return pl.pallas_call(
        _attention_kernel,
        grid=(b, N_KV_HEADS, n // tq),
        in_specs=[
            pl.BlockSpec((1, Q_PER_KV, HEAD_DIM, tq), lambda i, h, t: (i, h, 0, t)),
            pl.BlockSpec((1, n, K_W), lambda i, h, t: (i, 0, 0)),
            pl.BlockSpec((1, n_ctx, K_W), lambda i, h, t: (i, 0, 0)),
            pl.BlockSpec((1, 1, HEAD_DIM, n), lambda i, h, t: (i, h, 0, 0)),
            pl.BlockSpec((1, 1, HEAD_DIM, n_ctx), lambda i, h, t: (i, h, 0, 0)),
        ],
        out_specs=pl.BlockSpec((1, tq, Q_PER_KV * HEAD_DIM), lambda i, h, t: (i, t, h)),
        out_shape=jax.ShapeDtypeStruct((b, n, ATTN_WIDTH), BF16),
        scratch_shapes=[
            pltpu.VMEM((K_W, cols), BF16),
            pltpu.VMEM((1, cols), F32),
            pltpu.VMEM((HEAD_DIM + ONES_ROWS, cols), F32),
        ],
        compiler_params=pltpu.CompilerParams(
            dimension_semantics=("arbitrary", "arbitrary", "arbitrary"),
            vmem_limit_bytes=V7X_VMEM_LIMIT_BYTES),
        name="attention",
    )(qt, k, kc, vt, vct)


def _post_kernel(x_ref, mod_ref, attn_ref, u_ref, vg_ref, ga_ref, gb_ref, ws_ref, bs_ref,
                 wa_ref, wg_ref, wo_ref, g2_ref, w1_ref, w2_ref, o_ref):
    tm = x_ref.shape[1]
    mod = mod_ref[0]
    gate1, shift2, scale2, gate2 = mod[2:3], mod[3:4], mod[4:5], mod[5:6]

    lane = lax.broadcasted_iota(jnp.int32, (CHUNK, 2 * GM_GROUP_DIM), 1)
    first_half = lane < GM_GROUP_DIM
    vg = vg_ref[0]
    chunks = []
    for c in range(tm // CHUNK):
        cols = []
        for gp in range(GM_GROUPS // 2):
            rhs = vg[c * CHUNK:(c + 1) * CHUNK, gp * 128:(gp + 1) * 128]
            s0 = jnp.dot(ws_ref[2 * gp], rhs, preferred_element_type=F32)
            s1 = jnp.dot(ws_ref[2 * gp + 1], rhs, preferred_element_type=F32)
            cols.append(jnp.where(first_half, s0, s1))
        chunks.append(jnp.concatenate(cols, axis=1) + bs_ref[...])
    s = jnp.concatenate(chunks, axis=0)
    gm = (u_ref[0].astype(F32) * s).astype(BF16)

    y = (ga_ref[0].astype(F32) * jnp.dot(attn_ref[0], wa_ref[...], preferred_element_type=F32)
         + gb_ref[0].astype(F32) * jnp.dot(gm, wg_ref[...], preferred_element_type=F32))
    x1 = x_ref[0] + gate1 * jnp.dot(y.astype(BF16), wo_ref[...], preferred_element_type=F32)

    h2 = _modulate(x1, g2_ref[...], shift2, scale2).astype(BF16)
    f = jnp.maximum(jnp.dot(h2, w1_ref[...], preferred_element_type=F32), 0.0)
    f = (f * f).astype(BF16)
    o_ref[0] = x1 + gate2 * jnp.dot(f, w2_ref[...], preferred_element_type=F32)


def _post(x, mods3, attn, u, vg, ga, gb, gm_ws, bs_full, w_br_attn, w_br_gm, w_out, norm2_g, w_ff1, w_ff2):
    b, n, d = x.shape
    tm = TOK_TILE
    tok = lambda w: pl.BlockSpec((1, tm, w), lambda i, t: (i, t, 0))
    return pl.pallas_call(
        _post_kernel,
        grid=(b, n // tm),
        in_specs=[
            tok(d),
            pl.BlockSpec((1, N_MOD, d), lambda i, t: (i, 0, 0)),
            tok(ATTN_WIDTH), tok(GM_WIDTH), tok(GM_WIDTH), tok(d), tok(d),
            _const_spec((GM_GROUPS, CHUNK, CHUNK)),
            _const_spec((CHUNK, GM_WIDTH)),
            _const_spec((ATTN_WIDTH, d)),
            _const_spec((GM_WIDTH, d)),
            _const_spec((d, d)),
            _const_spec((1, d)),
            _const_spec((d, D_FF)),
            _const_spec((D_FF, d)),
        ],
        out_specs=tok(d),
        out_shape=jax.ShapeDtypeStruct((b, n, d), F32),
        compiler_params=pltpu.CompilerParams(
            dimension_semantics=("arbitrary", "arbitrary"), vmem_limit_bytes=V7X_VMEM_LIMIT_BYTES),
        name="post",
    )(x, mods3, attn, u, vg, ga, gb, gm_ws, bs_full, w_br_attn, w_br_gm, w_out, norm2_g, w_ff1, w_ff2)


def _rope_tables_t(n_tokens):
    rows = n_tokens // GRID_W
    row = jnp.repeat(jnp.arange(rows, dtype=F32), GRID_W)
    col = jnp.tile(jnp.arange(GRID_W, dtype=F32), rows)
    inv = ROPE_THETA ** (-jnp.arange(0, ROT_AXIS_DIM, 2, dtype=F32) / ROT_AXIS_DIM)
    ang_t = jnp.concatenate([inv[:, None] * row[None, :], inv[:, None] * col[None, :]], axis=0)
    return jnp.cos(ang_t), jnp.sin(ang_t)


def kernel(x, c, ctx, c_ctx, w_mod, b_mod, norm1_g, norm2_g, w_in, q_norm_g, k_norm_g, gm_norm_g,
           gm_ws, gm_bs, w_br_attn, w_br_gm, w_out, w_ff1, w_ff2):
    b, n, d = x.shape
    assert w_mod.shape[0] == 1, "single-layer block"
    assert (b, d) == c.shape and d == D_MODEL and n % max(TOK_TILE, Q_TILE, KV_TILE) == 0

    pad_rows = (-(b + 1)) % 8
    c_all = jnp.concatenate([c, c_ctx[None, :], jnp.zeros((pad_rows, d), c.dtype)], axis=0)
    mods3 = _adaln(c_all, w_mod[0], b_mod[0]).reshape(b + 1 + pad_rows, N_MOD, d)

    w_in_b = w_in[0].astype(BF16)
    k_norm_col = k_norm_g[0].reshape(HEAD_DIM, 1)
    q_norm_col = q_norm_g[0].reshape(HEAD_DIM, 1)
    gm_norm_row = gm_norm_g[0].reshape(1, GM_WIDTH)
    gid = jnp.arange(GM_WIDTH) // GM_GROUP_DIM
    group_ones = (gid[:, None] == gid[None, :]).astype(BF16)
    cos_t, sin_t = _rope_tables_t(n)

    kc, vct = _ctx_kv(ctx, mods3, norm1_g, w_in_b[:, :KV_COLS], k_norm_col, mod_row=b)
    k, vt, qt, u, vg, ga, gb = _in_proj(
        x, mods3, norm1_g, w_in_b, k_norm_col, q_norm_col, gm_norm_row, group_ones, cos_t, sin_t)
    attn = _attention(qt, k, kc, vt, vct)

    bs_full = jnp.repeat(gm_bs[0].T, GM_GROUP_DIM, axis=1)
    return _post(x, mods3, attn, u, vg, ga, gb, gm_ws[0].astype(BF16), bs_full,
                 w_br_attn[0].astype(BF16), w_br_gm[0].astype(BF16), w_out[0].astype(BF16),
                 norm2_g, w_ff1[0].astype(BF16), w_ff2[0].astype(BF16))
```

```python
import functools
import math

import jax
import jax.numpy as jnp
from jax import lax
from jax.experimental import pallas as pl
from jax.experimental.pallas import tpu as pltpu

F32 = jnp.float32
BF16 = jnp.bfloat16

D_MODEL = 1024
GRID_W = 64
HEAD_DIM = 64
N_Q_HEADS = 8
N_KV_HEADS = 2
Q_PER_KV = N_Q_HEADS // N_KV_HEADS
ATTN_WIDTH = N_Q_HEADS * HEAD_DIM
GM_GROUPS = 8
GM_GROUP_DIM = 64
GM_WIDTH = GM_GROUPS * GM_GROUP_DIM
CHUNK = 128
D_FF = 4 * D_MODEL
ROPE_THETA = 10000.0
ROT_AXIS_DIM = HEAD_DIM // 2
EPS = 1e-6
N_MOD = 6

K_W = N_KV_HEADS * HEAD_DIM
KV_COLS = 2 * K_W
Q_OFF = KV_COLS
U_OFF = Q_OFF + ATTN_WIDTH
VG_OFF = U_OFF + GM_WIDTH
GA_OFF = VG_OFF + GM_WIDTH
GB_OFF = GA_OFF + D_MODEL
D_IN = GB_OFF + D_MODEL

V7X_VMEM_LIMIT_BYTES = 56 * 1024 * 1024

TOK_TILE = 256
Q_TILE = 256
KV_TILE = 512
ONES_ROWS = 16

Q_SCALE_LOG2E = (HEAD_DIM ** -0.5) * math.log2(math.e)


def _sigmoid(x):
    return 1.0 / (1.0 + jnp.exp(-x))


def _gelu_tanh(x):
    c = math.sqrt(2.0 / math.pi)
    return 0.5 * x * (1.0 + jnp.tanh(c * (x + 0.044715 * (x * x * x))))


def _modulate(x, g, shift, scale):
    ms = jnp.mean(x * x, axis=-1, keepdims=True)
    return (x * lax.rsqrt(ms + EPS) * g) * (1.0 + scale) + shift


def _const_spec(shape):
    return pl.BlockSpec(shape, lambda *_: (0,) * len(shape), pipeline_mode=pl.Buffered(1))


def _adaln_kernel(c_ref, w_ref, b_ref, o_ref):
    c = c_ref[...]
    a = (c * _sigmoid(c)).astype(BF16)
    o_ref[...] = jnp.dot(a, w_ref[...].astype(BF16), preferred_element_type=F32) + b_ref[...]


def _adaln(c_all, w_mod, b_mod):
    rows, d = c_all.shape
    n = w_mod.shape[1]
    tn = 1024
    return pl.pallas_call(
        _adaln_kernel,
        grid=(n // tn,),
        in_specs=[
            pl.BlockSpec((rows, d), lambda j: (0, 0)),
            pl.BlockSpec((d, tn), lambda j: (0, j)),
            pl.BlockSpec((1, tn), lambda j: (0, j)),
        ],
        out_specs=pl.BlockSpec((rows, tn), lambda j: (0, j)),
        out_shape=jax.ShapeDtypeStruct((rows, n), F32),
        compiler_params=pltpu.CompilerParams(
            dimension_semantics=("arbitrary",), vmem_limit_bytes=V7X_VMEM_LIMIT_BYTES),
        name="adaln",
    )(c_all, w_mod, b_mod.reshape(1, n))


def _head_norm_t(blk, g_col):
    ms = jnp.mean(blk * blk, axis=0, keepdims=True)
    return blk * lax.rsqrt(ms + EPS) * g_col


def _rope_t(y, cos_t, sin_t):
    y1, y2 = y[:ROT_AXIS_DIM], y[ROT_AXIS_DIM:]
    return jnp.concatenate([y1 * cos_t - y2 * sin_t, y2 * cos_t + y1 * sin_t], axis=0)


def _ctx_kv_kernel(x_ref, mod_ref, g1_ref, w_ref, kg_ref, k_ref, vt_ref):
    x = x_ref[0]
    mod = mod_ref[0]
    h = _modulate(x, g1_ref[...], mod[0:1], mod[1:2]).astype(BF16)
    p = jnp.dot(h, w_ref[...], preferred_element_type=F32)
    pt = p.T
    kg = kg_ref[...]
    kt = jnp.concatenate(
        [_head_norm_t(pt[HEAD_DIM * j:HEAD_DIM * (j + 1)], kg) for j in range(N_KV_HEADS)], axis=0)
    k_ref[0] = kt.T.astype(BF16)
    for j in range(N_KV_HEADS):
        vt_ref[0, j] = pt[K_W + HEAD_DIM * j:K_W + HEAD_DIM * (j + 1)].astype(BF16)


def _ctx_kv(ctx, mods3, norm1_g, w_kv, k_norm_col, mod_row):
    b, n, d = ctx.shape
    return pl.pallas_call(
        _ctx_kv_kernel,
        grid=(b,),
        in_specs=[
            pl.BlockSpec((1, n, d), lambda i: (i, 0, 0)),
            pl.BlockSpec((1, N_MOD, d), lambda i: (mod_row, 0, 0)),
            _const_spec((1, d)),
            _const_spec((d, KV_COLS)),
            _const_spec((HEAD_DIM, 1)),
        ],
        out_specs=[
            pl.BlockSpec((1, n, K_W), lambda i: (i, 0, 0)),
            pl.BlockSpec((1, N_KV_HEADS, HEAD_DIM, n), lambda i: (i, 0, 0, 0)),
        ],
        out_shape=[
            jax.ShapeDtypeStruct((b, n, K_W), BF16),
            jax.ShapeDtypeStruct((b, N_KV_HEADS, HEAD_DIM, n), BF16),
        ],
        compiler_params=pltpu.CompilerParams(
            dimension_semantics=("arbitrary",), vmem_limit_bytes=V7X_VMEM_LIMIT_BYTES),
        name="ctx_kv",
    )(ctx, mods3, norm1_g, w_kv, k_norm_col)


def _in_proj_kernel(x_ref, mod_ref, g1_ref, w_ref, kg_ref, qg_ref, gmg_ref, e_ref, cos_ref, sin_ref,
                    k_ref, vt_ref, qt_ref, u_ref, vg_ref, ga_ref, gb_ref):
    x = x_ref[0]
    mod = mod_ref[0]
    h = _modulate(x, g1_ref[...], mod[0:1], mod[1:2]).astype(BF16)
    p = jnp.dot(h, w_ref[...], preferred_element_type=F32)

    cos_t = cos_ref[...]
    sin_t = sin_ref[...]
    pt = p[:, :U_OFF].T
    kg = kg_ref[...]
    kt = jnp.concatenate(
        [_rope_t(_head_norm_t(pt[HEAD_DIM * j:HEAD_DIM * (j + 1)], kg), cos_t, sin_t)
         for j in range(N_KV_HEADS)], axis=0)
    k_ref[0] = kt.T.astype(BF16)
    for j in range(N_KV_HEADS):
        vt_ref[0, j] = pt[K_W + HEAD_DIM * j:K_W + HEAD_DIM * (j + 1)].astype(BF16)
    qg = qg_ref[...]
    for j in range(N_Q_HEADS):
        blk = pt[Q_OFF + HEAD_DIM * j:Q_OFF + HEAD_DIM * (j + 1)]
        qt_ref[0, j] = (_rope_t(_head_norm_t(blk, qg), cos_t, sin_t) * Q_SCALE_LOG2E).astype(BF16)

    u_ref[0] = _gelu_tanh(p[:, U_OFF:VG_OFF]).astype(BF16)
    vg = _gelu_tanh(p[:, VG_OFF:GA_OFF])
    ss = jnp.dot((vg * vg).astype(BF16), e_ref[...], preferred_element_type=F32) * (1.0 / GM_GROUP_DIM)
    vg_ref[0] = (vg * lax.rsqrt(ss + EPS) * gmg_ref[...]).astype(BF16)
    ga_ref[0] = _sigmoid(p[:, GA_OFF:GB_OFF]).astype(BF16)
    gb_ref[0] = _sigmoid(p[:, GB_OFF:D_IN]).astype(BF16)


def _in_proj(x, mods3, norm1_g, w_in, k_norm_col, q_norm_col, gm_norm_row, group_ones, cos_t, sin_t):
    b, n, d = x.shape
    tm = TOK_TILE
    tok = lambda w: pl.BlockSpec((1, tm, w), lambda i, t: (i, t, 0))
    return pl.pallas_call(
        _in_proj_kernel,
        grid=(b, n // tm),
        in_specs=[
            tok(d),
            pl.BlockSpec((1, N_MOD, d), lambda i, t: (i, 0, 0)),
            _const_spec((1, d)),
            _const_spec((d, D_IN)),
            _const_spec((HEAD_DIM, 1)),
            _const_spec((HEAD_DIM, 1)),
            _const_spec((1, GM_WIDTH)),
            _const_spec((GM_WIDTH, GM_WIDTH)),
            pl.BlockSpec((ROT_AXIS_DIM, tm), lambda i, t: (0, t)),
            pl.BlockSpec((ROT_AXIS_DIM, tm), lambda i, t: (0, t)),
        ],
        out_specs=[
            tok(K_W),
            pl.BlockSpec((1, N_KV_HEADS, HEAD_DIM, tm), lambda i, t: (i, 0, 0, t)),
            pl.BlockSpec((1, N_Q_HEADS, HEAD_DIM, tm), lambda i, t: (i, 0, 0, t)),
            tok(GM_WIDTH), tok(GM_WIDTH), tok(d), tok(d),
        ],
        out_shape=[
            jax.ShapeDtypeStruct((b, n, K_W), BF16),
            jax.ShapeDtypeStruct((b, N_KV_HEADS, HEAD_DIM, n), BF16),
            jax.ShapeDtypeStruct((b, N_Q_HEADS, HEAD_DIM, n), BF16),
            jax.ShapeDtypeStruct((b, n, GM_WIDTH), BF16),
            jax.ShapeDtypeStruct((b, n, GM_WIDTH), BF16),
            jax.ShapeDtypeStruct((b, n, d), BF16),
            jax.ShapeDtypeStruct((b, n, d), BF16),
        ],
        compiler_params=pltpu.CompilerParams(
            dimension_semantics=("arbitrary", "arbitrary"), vmem_limit_bytes=V7X_VMEM_LIMIT_BYTES),
        name="in_proj",
    )(x, mods3, norm1_g, w_in, k_norm_col, q_norm_col, gm_norm_row, group_ones, cos_t, sin_t)


def _attention_kernel(qt_ref, k_ref, kc_ref, vt_ref, vct_ref, o_ref, qz_ref, m_ref, acc_ref, sa_ref, sb_ref):
    kv_head = pl.program_id(1)
    tq = qt_ref.shape[3]
    n_tiles = k_ref.shape[1] // KV_TILE

    q_cat = jnp.concatenate([qt_ref[0, g] for g in range(Q_PER_KV)], axis=1)
    zeros = jnp.zeros_like(q_cat)
    qz_ref[...] = jnp.concatenate(
        [jnp.where(kv_head == 0, q_cat, zeros), jnp.where(kv_head == 1, q_cat, zeros)], axis=0)

    def with_ones(vt):
        return jnp.concatenate([vt, jnp.ones((ONES_ROWS, vt.shape[1]), vt.dtype)], axis=0)

    def tile_start(i):
        return i * KV_TILE if isinstance(i, int) else pl.multiple_of(i * KV_TILE, KV_TILE)

    def produce(i, s_ref):
        s = jnp.dot(k_ref[0, pl.ds(tile_start(i), KV_TILE), :], qz_ref[...], preferred_element_type=F32)
        s_ref[...] = s
        return jnp.max(s, axis=0, keepdims=True)

    def consume(i, s_ref, s_max):
        m_old = m_ref[...]
        m_new = jnp.maximum(m_old, s_max)
        alpha = jnp.exp2(m_old - m_new)
        p = jnp.exp2(s_ref[...] - m_new).astype(BF16)
        v1t = with_ones(vt_ref[0, 0, :, pl.ds(tile_start(i), KV_TILE)])
        acc_ref[...] = alpha * acc_ref[...] + jnp.dot(v1t, p, preferred_element_type=F32)
        m_ref[...] = m_new

    s = jnp.dot(kc_ref[0], qz_ref[...], preferred_element_type=F32)
    m0 = jnp.max(s, axis=0, keepdims=True)
    p = jnp.exp2(s - m0).astype(BF16)
    m_ref[...] = m0
    acc_ref[...] = jnp.dot(with_ones(vct_ref[0, 0]), p, preferred_element_type=F32)

    max_a = produce(0, sa_ref)

    def two_tiles(j, max_a):
        max_b = produce(2 * j + 1, sb_ref)
        consume(2 * j, sa_ref, max_a)
        max_a = produce(2 * j + 2, sa_ref)
        consume(2 * j + 1, sb_ref, max_b)
        return max_a

    assert n_tiles % 2 == 0
    max_a = lax.fori_loop(0, n_tiles // 2 - 1, two_tiles, max_a)
    max_b = produce(n_tiles - 1, sb_ref)
    consume(n_tiles - 2, sa_ref, max_a)
    consume(n_tiles - 1, sb_ref, max_b)

    acc = acc_ref[...]
    o_t = acc[:HEAD_DIM] / acc[HEAD_DIM:HEAD_DIM + 1]
    pairs = []
    for g in range(0, Q_PER_KV, 2):
        two = jnp.concatenate([o_t[:, g * tq:(g + 1) * tq], o_t[:, (g + 1) * tq:(g + 2) * tq]], axis=0)
        pairs.append(two.T)
    o_ref[0] = jnp.concatenate(pairs, axis=1).astype(o_ref.dtype)


def _attention(qt, k, kc, vt, vct):
    b, _, _, n = qt.shape
    n_ctx = kc.shape[1]
    tq = Q_TILE
    cols = Q_PER_KV * tq
    return pl.pallas_call(
        _attention_kernel,
        grid=(b, N_KV_HEADS, n // tq),
        in_specs=[
            pl.BlockSpec((1, Q_PER_KV, HEAD_DIM, tq), lambda i, h, t: (i, h, 0, t)),
            pl.BlockSpec((1, n, K_W), lambda i, h, t: (i, 0, 0)),
            pl.BlockSpec((1, n_ctx, K_W), lambda i, h, t: (i, 0, 0)),
            pl.BlockSpec((1, 1, HEAD_DIM, n), lambda i, h, t: (i, h, 0, 0)),
            pl.BlockSpec((1, 1, HEAD_DIM, n_ctx), lambda i, h, t: (i, h, 0, 0)),
        ],
        out_specs=pl.BlockSpec((1, tq, Q_PER_KV * HEAD_DIM), lambda i, h, t: (i, t, h)),
        out_shape=jax.ShapeDtypeStruct((b, n, ATTN_WIDTH), BF16),
        scratch_shapes=[
            pltpu.VMEM((K_W, cols), BF16),
            pltpu.VMEM((1, cols), F32),
            pltpu.VMEM((HEAD_DIM + ONES_ROWS, cols), F32),
            pltpu.VMEM((KV_TILE, cols), F32),
            pltpu.VMEM((KV_TILE, cols), F32),
        ],
        compiler_params=pltpu.CompilerParams(
            dimension_semantics=("arbitrary", "arbitrary", "arbitrary"),
            vmem_limit_bytes=V7X_VMEM_LIMIT_BYTES),
        name="attention",
    )(qt, k, kc, vt, vct)


def _post_kernel(x_ref, mod_ref, attn_ref, u_ref, vg_ref, ga_ref, gb_ref, ws_ref, bs_ref,
                 wa_ref, wg_ref, wo_ref, g2_ref, w1_ref, w2_ref, o_ref):
    tm = x_ref.shape[1]
    mod = mod_ref[0]
    gate1, shift2, scale2, gate2 = mod[2:3], mod[3:4], mod[4:5], mod[5:6]

    lane = lax.broadcasted_iota(jnp.int32, (CHUNK, 2 * GM_GROUP_DIM), 1)
    first_half = lane < GM_GROUP_DIM
    vg = vg_ref[0]
    chunks = []
    for c in range(tm // CHUNK):
        cols = []
        for gp in range(GM_GROUPS // 2):
            rhs = vg[c * CHUNK:(c + 1) * CHUNK, gp * 128:(gp + 1) * 128]
            s0 = jnp.dot(ws_ref[2 * gp], rhs, preferred_element_type=F32)
            s1 = jnp.dot(ws_ref[2 * gp + 1], rhs, preferred_element_type=F32)
            cols.append(jnp.where(first_half, s0, s1))
        chunks.append(jnp.concatenate(cols, axis=1) + bs_ref[...])
    s = jnp.concatenate(chunks, axis=0)
    gm = (u_ref[0].astype(F32) * s).astype(BF16)

    y = (ga_ref[0].astype(F32) * jnp.dot(attn_ref[0], wa_ref[...], preferred_element_type=F32)
         + gb_ref[0].astype(F32) * jnp.dot(gm, wg_ref[...], preferred_element_type=F32))
    x1 = x_ref[0] + gate1 * jnp.dot(y.astype(BF16), wo_ref[...], preferred_element_type=F32)

    h2 = _modulate(x1, g2_ref[...], shift2, scale2).astype(BF16)
    f = jnp.maximum(jnp.dot(h2, w1_ref[...], preferred_element_type=F32), 0.0)
    f = (f * f).astype(BF16)
    o_ref[0] = x1 + gate2 * jnp.dot(f, w2_ref[...], preferred_element_type=F32)


def _post(x, mods3, attn, u, vg, ga, gb, gm_ws, bs_full, w_br_attn, w_br_gm, w_out, norm2_g, w_ff1, w_ff2):
    b, n, d = x.shape
    tm = TOK_TILE
    tok = lambda w: pl.BlockSpec((1, tm, w), lambda i, t: (i, t, 0))
    return pl.pallas_call(
        _post_kernel,
        grid=(b, n // tm),
        in_specs=[
            tok(d),
            pl.BlockSpec((1, N_MOD, d), lambda i, t: (i, 0, 0)),
            tok(ATTN_WIDTH), tok(GM_WIDTH), tok(GM_WIDTH), tok(d), tok(d),
            _const_spec((GM_GROUPS, CHUNK, CHUNK)),
            _const_spec((CHUNK, GM_WIDTH)),
            _const_spec((ATTN_WIDTH, d)),
            _const_spec((GM_WIDTH, d)),
            _const_spec((d, d)),
            _const_spec((1, d)),
            _const_spec((d, D_FF)),
            _const_spec((D_FF, d)),
        ],
        out_specs=tok(d),
        out_shape=jax.ShapeDtypeStruct((b, n, d), F32),
        compiler_params=pltpu.CompilerParams(
            dimension_semantics=("arbitrary", "arbitrary"), vmem_limit_bytes=V7X_VMEM_LIMIT_BYTES),
        name="post",
    )(x, mods3, attn, u, vg, ga, gb, gm_ws, bs_full, w_br_attn, w_br_gm, w_out, norm2_g, w_ff1, w_ff2)


def _rope_tables_t(n_tokens):
    rows = n_tokens // GRID_W
    row = jnp.repeat(jnp.arange(rows, dtype=F32), GRID_W)
    col = jnp.tile(jnp.arange(GRID_W, dtype=F32), rows)
    inv = ROPE_THETA ** (-jnp.arange(0, ROT_AXIS_DIM, 2, dtype=F32) / ROT_AXIS_DIM)
    ang_t = jnp.concatenate([inv[:, None] * row[None, :], inv[:, None] * col[None, :]], axis=0)
    return jnp.cos(ang_t), jnp.sin(ang_t)


def kernel(x, c, ctx, c_ctx, w_mod, b_mod, norm1_g, norm2_g, w_in, q_norm_g, k_norm_g, gm_norm_g,
           gm_ws, gm_bs, w_br_attn, w_br_gm, w_out, w_ff1, w_ff2):
    b, n, d = x.shape
    assert w_mod.shape[0] == 1, "single-layer block"
    assert (b, d) == c.shape and d == D_MODEL and n % max(TOK_TILE, Q_TILE, KV_TILE) == 0

    pad_rows = (-(b + 1)) % 8
    c_all = jnp.concatenate([c, c_ctx[None, :], jnp.zeros((pad_rows, d), c.dtype)], axis=0)
    mods3 = _adaln(c_all, w_mod[0], b_mod[0]).reshape(b + 1 + pad_rows, N_MOD, d)

    w_in_b = w_in[0].astype(BF16)
    k_norm_col = k_norm_g[0].reshape(HEAD_DIM, 1)
    q_norm_col = q_norm_g[0].reshape(HEAD_DIM, 1)
    gm_norm_row = gm_norm_g[0].reshape(1, GM_WIDTH)
    gid = jnp.arange(GM_WIDTH) // GM_GROUP_DIM
    group_ones = (gid[:, None] == gid[None, :]).astype(BF16)
    cos_t, sin_t = _rope_tables_t(n)

    kc, vct = _ctx_kv(ctx, mods3, norm1_g, w_in_b[:, :KV_COLS], k_norm_col, mod_row=b)
    k, vt, qt, u, vg, ga, gb = _in_proj(
        x, mods3, norm1_g, w_in_b, k_norm_col, q_norm_col, gm_norm_row, group_ones, cos_t, sin_t)
    attn = _attention(qt, k, kc, vt, vct)

    bs_full = jnp.repeat(gm_bs[0].T, GM_GROUP_DIM, axis=1)
    return _post(x, mods3, attn, u, vg, ga, gb, gm_ws[0].astype(BF16), bs_full,
                 w_br_attn[0].astype(BF16), w_br_gm[0].astype(BF16), w_out[0].astype(BF16),
                 norm2_g, w_ff1[0].astype(BF16), w_ff2[0].astype(BF16))
```

```python
import functools
import math

import jax
import jax.numpy as jnp
from jax import lax
from jax.experimental import pallas as pl
from jax.experimental.pallas import tpu as pltpu

F32 = jnp.float32
BF16 = jnp.bfloat16

D_MODEL = 1024
GRID_W = 64
HEAD_DIM = 64
N_Q_HEADS = 8
N_KV_HEADS = 2
Q_PER_KV = N_Q_HEADS // N_KV_HEADS
ATTN_WIDTH = N_Q_HEADS * HEAD_DIM
GM_GROUPS = 8
GM_GROUP_DIM = 64
GM_WIDTH = GM_GROUPS * GM_GROUP_DIM
CHUNK = 128
D_FF = 4 * D_MODEL
ROPE_THETA = 10000.0
ROT_AXIS_DIM = HEAD_DIM // 2
EPS = 1e-6
N_MOD = 6

K_W = N_KV_HEADS * HEAD_DIM
KV_COLS = 2 * K_W
Q_OFF = KV_COLS
U_OFF = Q_OFF + ATTN_WIDTH
VG_OFF = U_OFF + GM_WIDTH
GA_OFF = VG_OFF + GM_WIDTH
GB_OFF = GA_OFF + D_MODEL
D_IN = GB_OFF + D_MODEL

V7X_VMEM_LIMIT_BYTES = 56 * 1024 * 1024

TOK_TILE = 256
Q_TILE = 256
KV_TILE = 256
N_SCORE_BUFS = 4
STEPS_PER_TRIP = 12
ONES_ROWS = 16

Q_SCALE_LOG2E = (HEAD_DIM ** -0.5) * math.log2(math.e)


def _sigmoid(x):
    return 1.0 / (1.0 + jnp.exp(-x))


def _gelu_tanh(x):
    c = math.sqrt(2.0 / math.pi)
    return 0.5 * x * (1.0 + jnp.tanh(c * (x + 0.044715 * (x * x * x))))


def _modulate(x, g, shift, scale):
    ms = jnp.mean(x * x, axis=-1, keepdims=True)
    return (x * lax.rsqrt(ms + EPS) * g) * (1.0 + scale) + shift


def _const_spec(shape):
    return pl.BlockSpec(shape, lambda *_: (0,) * len(shape), pipeline_mode=pl.Buffered(1))


def _adaln_kernel(c_ref, w_ref, b_ref, o_ref):
    c = c_ref[...]
    a = (c * _sigmoid(c)).astype(BF16)
    o_ref[...] = jnp.dot(a, w_ref[...].astype(BF16), preferred_element_type=F32) + b_ref[...]


def _adaln(c_all, w_mod, b_mod):
    rows, d = c_all.shape
    n = w_mod.shape[1]
    tn = 1024
    return pl.pallas_call(
        _adaln_kernel,
        grid=(n // tn,),
        in_specs=[
            pl.BlockSpec((rows, d), lambda j: (0, 0)),
            pl.BlockSpec((d, tn), lambda j: (0, j)),
            pl.BlockSpec((1, tn), lambda j: (0, j)),
        ],
        out_specs=pl.BlockSpec((rows, tn), lambda j: (0, j)),
        out_shape=jax.ShapeDtypeStruct((rows, n), F32),
        compiler_params=pltpu.CompilerParams(
            dimension_semantics=("arbitrary",), vmem_limit_bytes=V7X_VMEM_LIMIT_BYTES),
        name="adaln",
    )(c_all, w_mod, b_mod.reshape(1, n))


def _head_norm_t(blk, g_col):
    ms = jnp.mean(blk * blk, axis=0, keepdims=True)
    return blk * lax.rsqrt(ms + EPS) * g_col


def _rope_t(y, cos_t, sin_t):
    y1, y2 = y[:ROT_AXIS_DIM], y[ROT_AXIS_DIM:]
    return jnp.concatenate([y1 * cos_t - y2 * sin_t, y2 * cos_t + y1 * sin_t], axis=0)


def _in_proj_kernel(x_ref, ctx_ref, mod_ref, g1_ref, w_ref, kg_ref, qg_ref, gmg_ref, e_ref, cos_ref, sin_ref,
                    k_ref, vt_ref, qt_ref, u_ref, vg_ref, ga_ref, gb_ref):
    x = jnp.where(pl.program_id(1) == 0, ctx_ref[0], x_ref[0])
    mod = mod_ref[0]
    h = _modulate(x, g1_ref[...], mod[0:1], mod[1:2]).astype(BF16)
    p = jnp.dot(h, w_ref[...], preferred_element_type=F32)

    cos_t = cos_ref[...]
    sin_t = sin_ref[...]
    pt = p[:, :U_OFF].T
    kg = kg_ref[...]
    kt = jnp.concatenate(
        [_rope_t(_head_norm_t(pt[HEAD_DIM * j:HEAD_DIM * (j + 1)], kg), cos_t, sin_t)
         for j in range(N_KV_HEADS)], axis=0)
    k_ref[0] = kt.T.astype(BF16)
    for j in range(N_KV_HEADS):
        vt_ref[0, j] = pt[K_W + HEAD_DIM * j:K_W + HEAD_DIM * (j + 1)].astype(BF16)
    qg = qg_ref[...]
    for j in range(N_Q_HEADS):
        blk = pt[Q_OFF + HEAD_DIM * j:Q_OFF + HEAD_DIM * (j + 1)]
        qt_ref[0, j] = (_rope_t(_head_norm_t(blk, qg), cos_t, sin_t) * Q_SCALE_LOG2E).astype(BF16)

    u_ref[0] = _gelu_tanh(p[:, U_OFF:VG_OFF]).astype(BF16)
    vg = _gelu_tanh(p[:, VG_OFF:GA_OFF])
    ss = jnp.dot((vg * vg).astype(BF16), e_ref[...], preferred_element_type=F32) * (1.0 / GM_GROUP_DIM)
    vg_ref[0] = (vg * lax.rsqrt(ss + EPS) * gmg_ref[...]).astype(BF16)
    ga_ref[0] = _sigmoid(p[:, GA_OFF:GB_OFF]).astype(BF16)
    gb_ref[0] = _sigmoid(p[:, GB_OFF:D_IN]).astype(BF16)


def _in_proj(x, ctx, mods3, norm1_g, w_in, k_norm_col, q_norm_col, gm_norm_row, group_ones, cos_t, sin_t):
    b, n, d = x.shape
    tm = TOK_TILE
    assert ctx.shape == (b, tm, d)
    n_keys = n + tm
    lat = lambda t: jnp.maximum(t - 1, 0)
    tok = lambda w: pl.BlockSpec((1, tm, w), lambda i, t: (i, lat(t), 0))
    return pl.pallas_call(
        _in_proj_kernel,
        grid=(b, n // tm + 1),
        in_specs=[
            tok(d),
            pl.BlockSpec((1, tm, d), lambda i, t: (i, 0, 0)),
            pl.BlockSpec((1, N_MOD, d), lambda i, t: (jnp.where(t == 0, b, i), 0, 0)),
            _const_spec((1, d)),
            _const_spec((d, D_IN)),
            _const_spec((HEAD_DIM, 1)),
            _const_spec((HEAD_DIM, 1)),
            _const_spec((1, GM_WIDTH)),
            _const_spec((GM_WIDTH, GM_WIDTH)),
            pl.BlockSpec((ROT_AXIS_DIM, tm), lambda i, t: (0, t)),
            pl.BlockSpec((ROT_AXIS_DIM, tm), lambda i, t: (0, t)),
        ],
        out_specs=[
            pl.BlockSpec((1, tm, K_W), lambda i, t: (i, t, 0)),
            pl.BlockSpec((1, N_KV_HEADS, HEAD_DIM, tm), lambda i, t: (i, 0, 0, t)),
            pl.BlockSpec((1, N_Q_HEADS, HEAD_DIM, tm), lambda i, t: (i, 0, 0, lat(t))),
            tok(GM_WIDTH), tok(GM_WIDTH), tok(d), tok(d),
        ],
        out_shape=[
            jax.ShapeDtypeStruct((b, n_keys, K_W), BF16),
            jax.ShapeDtypeStruct((b, N_KV_HEADS, HEAD_DIM, n_keys), BF16),
            jax.ShapeDtypeStruct((b, N_Q_HEADS, HEAD_DIM, n), BF16),
            jax.ShapeDtypeStruct((b, n, GM_WIDTH), BF16),
            jax.ShapeDtypeStruct((b, n, GM_WIDTH), BF16),
            jax.ShapeDtypeStruct((b, n, d), BF16),
            jax.ShapeDtypeStruct((b, n, d), BF16),
        ],
        compiler_params=pltpu.CompilerParams(
            dimension_semantics=("arbitrary", "arbitrary"), vmem_limit_bytes=V7X_VMEM_LIMIT_BYTES),
        name="in_proj",
    )(x, ctx, mods3, norm1_g, w_in, k_norm_col, q_norm_col, gm_norm_row, group_ones, cos_t, sin_t)


def _attention_kernel(qt_ref, k_ref, vt_ref, o_ref, qz_ref, m_ref, acc_ref, *s_refs):
    kv_head = pl.program_id(1)
    tq = qt_ref.shape[3]
    n_tiles = k_ref.shape[1] // KV_TILE

    q_cat = jnp.concatenate([qt_ref[0, g] for g in range(Q_PER_KV)], axis=1)
    zeros = jnp.zeros_like(q_cat)
    qz_ref[...] = jnp.concatenate(
        [jnp.where(kv_head == 0, q_cat, zeros), jnp.where(kv_head == 1, q_cat, zeros)], axis=0)

    def tile_slice(i):
        return pl.ds(i * KV_TILE if isinstance(i, int) else pl.multiple_of(i * KV_TILE, KV_TILE), KV_TILE)

    def produce(i, buf):
        s = jnp.dot(k_ref[0, tile_slice(i), :], qz_ref[...], preferred_element_type=F32)
        s_refs[buf][...] = s
        return jnp.max(s, axis=0, keepdims=True)

    def consume(i, buf, s_max):
        vt = vt_ref[0, 0, :, tile_slice(i)]
        v1t = jnp.concatenate([vt, jnp.ones((ONES_ROWS, KV_TILE), vt.dtype)], axis=0)
        m_old = m_ref[...]
        m_new = jnp.maximum(m_old, s_max)
        p = jnp.exp2(s_refs[buf][...] - m_new).astype(BF16)
        pv = jnp.dot(v1t, p, preferred_element_type=F32)
        acc_ref[...] = jnp.exp2(m_old - m_new) * acc_ref[...] + pv
        m_ref[...] = m_new

    m_ref[...] = jnp.full(m_ref.shape, -jnp.inf, F32)
    acc_ref[...] = jnp.zeros(acc_ref.shape, F32)

    def steps(first_tile, count, maxes):
        max_0, max_1 = maxes
        for j in range(count):
            max_2 = produce(first_tile + j + 2, (j + 2) % N_SCORE_BUFS)
            consume(first_tile + j, j % N_SCORE_BUFS, max_0)
            max_0, max_1 = max_1, max_2
        return max_0, max_1

    assert STEPS_PER_TRIP % N_SCORE_BUFS == 0
    maxes = (produce(0, 0), produce(1, 1))
    n_steps = n_tiles - 2
    n_trips = n_steps // STEPS_PER_TRIP
    maxes = lax.fori_loop(0, n_trips, lambda t, mx: steps(t * STEPS_PER_TRIP, STEPS_PER_TRIP, mx), maxes)
    done = n_trips * STEPS_PER_TRIP
    maxes = steps(done, n_steps - done, maxes)
    for j in range(2):
        consume(n_steps + j, (n_steps + j) % N_SCORE_BUFS, maxes[j])

    acc = acc_ref[...]
    o_t = acc[:HEAD_DIM] / acc[HEAD_DIM:HEAD_DIM + 1]
    pairs = []
    for g in range(0, Q_PER_KV, 2):
        two = jnp.concatenate([o_t[:, g * tq:(g + 1) * tq], o_t[:, (g + 1) * tq:(g + 2) * tq]], axis=0)
        pairs.append(two.T)
    o_ref[0] = jnp.concatenate(pairs, axis=1).astype(o_ref.dtype)


def _attention(qt, k, vt):
    b, _, _, n = qt.shape
    n_keys = k.shape[1]
    assert n_keys % KV_TILE == 0
    tq = Q_TILE
    cols = Q_PER_KV * tq
    return pl.pallas_call(
        _attention_kernel,
        grid=(b, N_KV_HEADS, n // tq),
        in_specs=[
            pl.BlockSpec((1, Q_PER_KV, HEAD_DIM, tq), lambda i, h, t: (i, h, 0, t)),
            pl.BlockSpec((1, n_keys, K_W), lambda i, h, t: (i, 0, 0)),
            pl.BlockSpec((1, 1, HEAD_DIM, n_keys), lambda i, h, t: (i, h, 0, 0)),
        ],
        out_specs=pl.BlockSpec((1, tq, Q_PER_KV * HEAD_DIM), lambda i, h, t: (i, t, h)),
        out_shape=jax.ShapeDtypeStruct((b, n, ATTN_WIDTH), BF16),
        scratch_shapes=[
            pltpu.VMEM((K_W, cols), BF16),
            pltpu.VMEM((1, cols), F32),
            pltpu.VMEM((HEAD_DIM + ONES_ROWS, cols), F32),
        ] + [pltpu.VMEM((KV_TILE, cols), F32)] * N_SCORE_BUFS,
        compiler_params=pltpu.CompilerParams(
            dimension_semantics=("arbitrary", "arbitrary", "arbitrary"),
            vmem_limit_bytes=V7X_VMEM_LIMIT_BYTES),
        name="attention",
    )(qt, k, vt)


def _post_kernel(x_ref, mod_ref, attn_ref, u_ref, vg_ref, ga_ref, gb_ref, ws_ref, bs_ref,
                 wa_ref, wg_ref, wo_ref, g2_ref, w1_ref, w2_ref, o_ref):
    tm = x_ref.shape[1]
    mod = mod_ref[0]
    gate1, shift2, scale2, gate2 = mod[2:3], mod[3:4], mod[4:5], mod[5:6]

    lane = lax.broadcasted_iota(jnp.int32, (CHUNK, 2 * GM_GROUP_DIM), 1)
    first_half = lane < GM_GROUP_DIM
    vg = vg_ref[0]
    chunks = []
    for c in range(tm // CHUNK):
        cols = []
        for gp in range(GM_GROUPS // 2):
            rhs = vg[c * CHUNK:(c + 1) * CHUNK, gp * 128:(gp + 1) * 128]
            s0 = jnp.dot(ws_ref[2 * gp], rhs, preferred_element_type=F32)
            s1 = jnp.dot(ws_ref[2 * gp + 1], rhs, preferred_element_type=F32)
            cols.append(jnp.where(first_half, s0, s1))
        chunks.append(jnp.concatenate(cols, axis=1) + bs_ref[...])
    s = jnp.concatenate(chunks, axis=0)
    gm = (u_ref[0].astype(F32) * s).astype(BF16)

    y = (ga_ref[0].astype(F32) * jnp.dot(attn_ref[0], wa_ref[...], preferred_element_type=F32)
         + gb_ref[0].astype(F32) * jnp.dot(gm, wg_ref[...], preferred_element_type=F32))
    x1 = x_ref[0] + gate1 * jnp.dot(y.astype(BF16), wo_ref[...], preferred_element_type=F32)

    h2 = _modulate(x1, g2_ref[...], shift2, scale2).astype(BF16)
    f = jnp.maximum(jnp.dot(h2, w1_ref[...], preferred_element_type=F32), 0.0)
    f = (f * f).astype(BF16)
    o_ref[0] = x1 + gate2 * jnp.dot(f, w2_ref[...], preferred_element_type=F32)


def _post(x, mods3, attn, u, vg, ga, gb, gm_ws, bs_full, w_br_attn, w_br_gm, w_out, norm2_g, w_ff1, w_ff2):
    b, n, d = x.shape
    tm = TOK_TILE
    tok = lambda w: pl.BlockSpec((1, tm, w), lambda i, t: (i, t, 0))
    return pl.pallas_call(
        _post_kernel,
        grid=(b, n // tm),
        in_specs=[
            tok(d),
            pl.BlockSpec((1, N_MOD, d), lambda i, t: (i, 0, 0)),
            tok(ATTN_WIDTH), tok(GM_WIDTH), tok(GM_WIDTH), tok(d), tok(d),
            _const_spec((GM_GROUPS, CHUNK, CHUNK)),
            _const_spec((CHUNK, GM_WIDTH)),
            _const_spec((ATTN_WIDTH, d)),
            _const_spec((GM_WIDTH, d)),
            _const_spec((d, d)),
            _const_spec((1, d)),
            _const_spec((d, D_FF)),
            _const_spec((D_FF, d)),
        ],
        out_specs=tok(d),
        out_shape=jax.ShapeDtypeStruct((b, n, d), F32),
        compiler_params=pltpu.CompilerParams(
            dimension_semantics=("arbitrary", "arbitrary"), vmem_limit_bytes=V7X_VMEM_LIMIT_BYTES),
        name="post",
    )(x, mods3, attn, u, vg, ga, gb, gm_ws, bs_full, w_br_attn, w_br_gm, w_out, norm2_g, w_ff1, w_ff2)


def _rope_tables_t(n_tokens, n_ctx):
    rows = n_tokens // GRID_W
    row = jnp.repeat(jnp.arange(rows, dtype=F32), GRID_W)
    col = jnp.tile(jnp.arange(GRID_W, dtype=F32), rows)
    inv = ROPE_THETA ** (-jnp.arange(0, ROT_AXIS_DIM, 2, dtype=F32) / ROT_AXIS_DIM)
    ang_t = jnp.concatenate([inv[:, None] * row[None, :], inv[:, None] * col[None, :]], axis=0)
    cos_t = jnp.concatenate([jnp.ones((ROT_AXIS_DIM, n_ctx), F32), jnp.cos(ang_t)], axis=1)
    sin_t = jnp.concatenate([jnp.zeros((ROT_AXIS_DIM, n_ctx), F32), jnp.sin(ang_t)], axis=1)
    return cos_t, sin_t


def kernel(x, c, ctx, c_ctx, w_mod, b_mod, norm1_g, norm2_g, w_in, q_norm_g, k_norm_g, gm_norm_g,
           gm_ws, gm_bs, w_br_attn, w_br_gm, w_out, w_ff1, w_ff2):
    b, n, d = x.shape
    assert w_mod.shape[0] == 1, "single-layer block"
    assert (b, d) == c.shape and d == D_MODEL and n % max(TOK_TILE, Q_TILE) == 0

    pad_rows = (-(b + 1)) % 8
    c_all = jnp.concatenate([c, c_ctx[None, :], jnp.zeros((pad_rows, d), c.dtype)], axis=0)
    mods3 = _adaln(c_all, w_mod[0], b_mod[0]).reshape(b + 1 + pad_rows, N_MOD, d)

    w_in_b = w_in[0].astype(BF16)
    k_norm_col = k_norm_g[0].reshape(HEAD_DIM, 1)
    q_norm_col = q_norm_g[0].reshape(HEAD_DIM, 1)
    gm_norm_row = gm_norm_g[0].reshape(1, GM_WIDTH)
    gid = jnp.arange(GM_WIDTH) // GM_GROUP_DIM
    group_ones = (gid[:, None] == gid[None, :]).astype(BF16)
    cos_t, sin_t = _rope_tables_t(n, ctx.shape[1])

    k, vt, qt, u, vg, ga, gb = _in_proj(
        x, ctx, mods3, norm1_g, w_in_b, k_norm_col, q_norm_col, gm_norm_row, group_ones, cos_t, sin_t)
    attn = _attention(qt, k, vt)

    bs_full = jnp.repeat(gm_bs[0].T, GM_GROUP_DIM, axis=1)
    return _post(x, mods3, attn, u, vg, ga, gb, gm_ws[0].astype(BF16), bs_full,
                 w_br_attn[0].astype(BF16), w_br_gm[0].astype(BF16), w_out[0].astype(BF16),
                 norm2_g, w_ff1[0].astype(BF16), w_ff2[0].astype(BF16))
```

```python
import functools
import math

import jax
import jax.numpy as jnp
from jax import lax
from jax.experimental import pallas as pl
from jax.experimental.pallas import tpu as pltpu

F32 = jnp.float32
BF16 = jnp.bfloat16

D_MODEL = 1024
GRID_W = 64
HEAD_DIM = 64
N_Q_HEADS = 8
N_KV_HEADS = 2
Q_PER_KV = N_Q_HEADS // N_KV_HEADS
ATTN_WIDTH = N_Q_HEADS * HEAD_DIM
GM_GROUPS = 8
GM_GROUP_DIM = 64
GM_WIDTH = GM_GROUPS * GM_GROUP_DIM
CHUNK = 128
D_FF = 4 * D_MODEL
ROPE_THETA = 10000.0
ROT_AXIS_DIM = HEAD_DIM // 2
EPS = 1e-6
N_MOD = 6

K_W = N_KV_HEADS * HEAD_DIM
KV_COLS = 2 * K_W
Q_OFF = KV_COLS
U_OFF = Q_OFF + ATTN_WIDTH
VG_OFF = U_OFF + GM_WIDTH
GA_OFF = VG_OFF + GM_WIDTH
GB_OFF = GA_OFF + D_MODEL
D_IN = GB_OFF + D_MODEL

V7X_VMEM_LIMIT_BYTES = 56 * 1024 * 1024

TOK_TILE = 256
POST_TILE = 512
Q_TILE = 256
KV_TILE = 256
N_SCORE_BUFS = 4
STEPS_PER_TRIP = 12
ONES_ROWS = 16

Q_SCALE_LOG2E = (HEAD_DIM ** -0.5) * math.log2(math.e)


def _sigmoid(x):
    return 1.0 / (1.0 + jnp.exp(-x))


def _gelu_tanh(x):
    c = math.sqrt(2.0 / math.pi)
    return 0.5 * x * (1.0 + jnp.tanh(c * (x + 0.044715 * (x * x * x))))


def _modulate(x, g, shift, scale):
    ms = jnp.mean(x * x, axis=-1, keepdims=True)
    return (x * lax.rsqrt(ms + EPS) * g) * (1.0 + scale) + shift


def _const_spec(shape):
    return pl.BlockSpec(shape, lambda *_: (0,) * len(shape), pipeline_mode=pl.Buffered(1))


def _adaln_kernel(c_ref, w_ref, b_ref, o_ref):
    c = c_ref[...]
    a = (c * _sigmoid(c)).astype(BF16)
    o_ref[...] = jnp.dot(a, w_ref[...].astype(BF16), preferred_element_type=F32) + b_ref[...]


def _adaln(c_all, w_mod, b_mod):
    rows, d = c_all.shape
    n = w_mod.shape[1]
    tn = 1024
    return pl.pallas_call(
        _adaln_kernel,
        grid=(n // tn,),
        in_specs=[
            pl.BlockSpec((rows, d), lambda j: (0, 0)),
            pl.BlockSpec((d, tn), lambda j: (0, j)),
            pl.BlockSpec((1, tn), lambda j: (0, j)),
        ],
        out_specs=pl.BlockSpec((rows, tn), lambda j: (0, j)),
        out_shape=jax.ShapeDtypeStruct((rows, n), F32),
        compiler_params=pltpu.CompilerParams(
            dimension_semantics=("arbitrary",), vmem_limit_bytes=V7X_VMEM_LIMIT_BYTES),
        name="adaln",
    )(c_all, w_mod, b_mod.reshape(1, n))


def _head_norm_t(blk, g_col):
    ms = jnp.mean(blk * blk, axis=0, keepdims=True)
    return blk * lax.rsqrt(ms + EPS) * g_col


def _rope_t(y, cos_t, sin_t):
    y1, y2 = y[:ROT_AXIS_DIM], y[ROT_AXIS_DIM:]
    return jnp.concatenate([y1 * cos_t - y2 * sin_t, y2 * cos_t + y1 * sin_t], axis=0)


def _in_proj_kernel(x_ref, ctx_ref, mod_ref, g1_ref, w_ref, kg_ref, qg_ref, gmg_ref, e_ref, cos_ref, sin_ref,
                    k_ref, vt_ref, qt_ref, u_ref, vg_ref, ga_ref, gb_ref):
    x = jnp.where(pl.program_id(1) == 0, ctx_ref[0], x_ref[0])
    mod = mod_ref[0]
    h = _modulate(x, g1_ref[...], mod[0:1], mod[1:2]).astype(BF16)
    p = jnp.dot(h, w_ref[...], preferred_element_type=F32)

    cos_t = cos_ref[...]
    sin_t = sin_ref[...]
    pt = p[:, :U_OFF].T
    kg = kg_ref[...]
    kt = jnp.concatenate(
        [_rope_t(_head_norm_t(pt[HEAD_DIM * j:HEAD_DIM * (j + 1)], kg), cos_t, sin_t)
         for j in range(N_KV_HEADS)], axis=0)
    k_ref[0] = kt.T.astype(BF16)
    for j in range(N_KV_HEADS):
        vt_ref[0, j] = pt[K_W + HEAD_DIM * j:K_W + HEAD_DIM * (j + 1)].astype(BF16)
    qg = qg_ref[...]
    for j in range(N_Q_HEADS):
        blk = pt[Q_OFF + HEAD_DIM * j:Q_OFF + HEAD_DIM * (j + 1)]
        qt_ref[0, j] = (_rope_t(_head_norm_t(blk, qg), cos_t, sin_t) * Q_SCALE_LOG2E).astype(BF16)

    u_ref[0] = _gelu_tanh(p[:, U_OFF:VG_OFF]).astype(BF16)
    vg = _gelu_tanh(p[:, VG_OFF:GA_OFF])
    ss = jnp.dot((vg * vg).astype(BF16), e_ref[...], preferred_element_type=F32) * (1.0 / GM_GROUP_DIM)
    vg_ref[0] = (vg * lax.rsqrt(ss + EPS) * gmg_ref[...]).astype(BF16)
    ga_ref[0] = _sigmoid(p[:, GA_OFF:GB_OFF]).astype(BF16)
    gb_ref[0] = _sigmoid(p[:, GB_OFF:D_IN]).astype(BF16)


def _in_proj(x, ctx, mods3, norm1_g, w_in, k_norm_col, q_norm_col, gm_norm_row, group_ones, cos_t, sin_t):
    b, n, d = x.shape
    tm = TOK_TILE
    assert ctx.shape == (b, tm, d)
    n_keys = n + tm
    lat = lambda t: jnp.maximum(t - 1, 0)
    tok = lambda w: pl.BlockSpec((1, tm, w), lambda i, t: (i, lat(t), 0))
    return pl.pallas_call(
        _in_proj_kernel,
        grid=(b, n // tm + 1),
        in_specs=[
            tok(d),
            pl.BlockSpec((1, tm, d), lambda i, t: (i, 0, 0)),
            pl.BlockSpec((1, N_MOD, d), lambda i, t: (jnp.where(t == 0, b, i), 0, 0)),
            _const_spec((1, d)),
            _const_spec((d, D_IN)),
            _const_spec((HEAD_DIM, 1)),
            _const_spec((HEAD_DIM, 1)),
            _const_spec((1, GM_WIDTH)),
            _const_spec((GM_WIDTH, GM_WIDTH)),
            pl.BlockSpec((ROT_AXIS_DIM, tm), lambda i, t: (0, t)),
            pl.BlockSpec((ROT_AXIS_DIM, tm), lambda i, t: (0, t)),
        ],
        out_specs=[
            pl.BlockSpec((1, tm, K_W), lambda i, t: (i, t, 0)),
            pl.BlockSpec((1, N_KV_HEADS, HEAD_DIM, tm), lambda i, t: (i, 0, 0, t)),
            pl.BlockSpec((1, N_Q_HEADS, HEAD_DIM, tm), lambda i, t: (i, 0, 0, lat(t))),
            tok(GM_WIDTH), tok(GM_WIDTH), tok(d), tok(d),
        ],
        out_shape=[
            jax.ShapeDtypeStruct((b, n_keys, K_W), BF16),
            jax.ShapeDtypeStruct((b, N_KV_HEADS, HEAD_DIM, n_keys), BF16),
            jax.ShapeDtypeStruct((b, N_Q_HEADS, HEAD_DIM, n), BF16),
            jax.ShapeDtypeStruct((b, n, GM_WIDTH), BF16),
            jax.ShapeDtypeStruct((b, n, GM_WIDTH), BF16),
            jax.ShapeDtypeStruct((b, n, d), BF16),
            jax.ShapeDtypeStruct((b, n, d), BF16),
        ],
        compiler_params=pltpu.CompilerParams(
            dimension_semantics=("arbitrary", "arbitrary"), vmem_limit_bytes=V7X_VMEM_LIMIT_BYTES),
        name="in_proj",
    )(x, ctx, mods3, norm1_g, w_in, k_norm_col, q_norm_col, gm_norm_row, group_ones, cos_t, sin_t)


def _attention_kernel(qt_ref, k_ref, vt_ref, o_ref, qz_ref, m_ref, acc_ref, *s_refs):
    def one_kv_head(kv_head, carry):
        _attention_kv_head(kv_head, qt_ref, k_ref, vt_ref, o_ref, qz_ref, m_ref, acc_ref, s_refs)
        return carry

    lax.fori_loop(0, N_KV_HEADS, one_kv_head, 0)


def _attention_kv_head(kv_head, qt_ref, k_ref, vt_ref, o_ref, qz_ref, m_ref, acc_ref, s_refs):
    tq = qt_ref.shape[3]
    n_tiles = k_ref.shape[1] // KV_TILE

    q_cat = jnp.concatenate([qt_ref[0, Q_PER_KV * kv_head + g] for g in range(Q_PER_KV)], axis=1)
    zeros = jnp.zeros_like(q_cat)
    qz_ref[...] = jnp.concatenate(
        [jnp.where(kv_head == 0, q_cat, zeros), jnp.where(kv_head == 1, q_cat, zeros)], axis=0)

    def tile_slice(i):
        return pl.ds(i * KV_TILE if isinstance(i, int) else pl.multiple_of(i * KV_TILE, KV_TILE), KV_TILE)

    def produce(i, buf):
        s = jnp.dot(k_ref[0, tile_slice(i), :], qz_ref[...], preferred_element_type=F32)
        s_refs[buf][...] = s
        return jnp.max(s, axis=0, keepdims=True)

    def consume(i, buf, s_max):
        vt = vt_ref[0, kv_head, :, tile_slice(i)]
        v1t = jnp.concatenate([vt, jnp.ones((ONES_ROWS, KV_TILE), vt.dtype)], axis=0)
        m_old = m_ref[...]
        m_new = jnp.maximum(m_old, s_max)
        p = jnp.exp2(s_refs[buf][...] - m_new).astype(BF16)
        pv = jnp.dot(v1t, p, preferred_element_type=F32)
        acc_ref[...] = jnp.exp2(m_old - m_new) * acc_ref[...] + pv
        m_ref[...] = m_new

    m_ref[...] = jnp.full(m_ref.shape, -jnp.inf, F32)
    acc_ref[...] = jnp.zeros(acc_ref.shape, F32)

    def steps(first_tile, count, maxes):
        max_0, max_1 = maxes
        for j in range(count):
            max_2 = produce(first_tile + j + 2, (j + 2) % N_SCORE_BUFS)
            consume(first_tile + j, j % N_SCORE_BUFS, max_0)
            max_0, max_1 = max_1, max_2
        return max_0, max_1

    assert STEPS_PER_TRIP % N_SCORE_BUFS == 0
    maxes = (produce(0, 0), produce(1, 1))
    n_steps = n_tiles - 2
    n_trips = n_steps // STEPS_PER_TRIP
    maxes = lax.fori_loop(0, n_trips, lambda t, mx: steps(t * STEPS_PER_TRIP, STEPS_PER_TRIP, mx), maxes)
    done = n_trips * STEPS_PER_TRIP
    maxes = steps(done, n_steps - done, maxes)
    for j in range(2):
        consume(n_steps + j, (n_steps + j) % N_SCORE_BUFS, maxes[j])

    acc = acc_ref[...]
    o_t = acc[:HEAD_DIM] / acc[HEAD_DIM:HEAD_DIM + 1]
    pairs = []
    for g in range(0, Q_PER_KV, 2):
        two = jnp.concatenate([o_t[:, g * tq:(g + 1) * tq], o_t[:, (g + 1) * tq:(g + 2) * tq]], axis=0)
        pairs.append(two.T)
    width = Q_PER_KV * HEAD_DIM
    o_ref[0, :, pl.ds(pl.multiple_of(kv_head * width, width), width)] = (
        jnp.concatenate(pairs, axis=1).astype(o_ref.dtype))


def _attention(qt, k, vt):
    b, _, _, n = qt.shape
    n_keys = k.shape[1]
    assert n_keys % KV_TILE == 0
    tq = Q_TILE
    cols = Q_PER_KV * tq
    return pl.pallas_call(
        _attention_kernel,
        grid=(b, n // tq),
        in_specs=[
            pl.BlockSpec((1, N_Q_HEADS, HEAD_DIM, tq), lambda i, t: (i, 0, 0, t)),
            pl.BlockSpec((1, n_keys, K_W), lambda i, t: (i, 0, 0)),
            pl.BlockSpec((1, N_KV_HEADS, HEAD_DIM, n_keys), lambda i, t: (i, 0, 0, 0)),
        ],
        out_specs=pl.BlockSpec((1, tq, ATTN_WIDTH), lambda i, t: (i, t, 0)),
        out_shape=jax.ShapeDtypeStruct((b, n, ATTN_WIDTH), BF16),
        scratch_shapes=[
            pltpu.VMEM((K_W, cols), BF16),
            pltpu.VMEM((1, cols), F32),
            pltpu.VMEM((HEAD_DIM + ONES_ROWS, cols), F32),
        ] + [pltpu.VMEM((KV_TILE, cols), F32)] * N_SCORE_BUFS,
        compiler_params=pltpu.CompilerParams(
            dimension_semantics=("arbitrary", "arbitrary"),
            vmem_limit_bytes=V7X_VMEM_LIMIT_BYTES),
        name="attention",
    )(qt, k, vt)


def _post_kernel(x_ref, mod_ref, attn_ref, u_ref, vg_ref, ga_ref, gb_ref, ws_ref, bs_ref,
                 wa_ref, wg_ref, wo_ref, g2_ref, w1_ref, w2_ref, o_ref):
    tm = x_ref.shape[1]
    mod = mod_ref[0]
    gate1, shift2, scale2, gate2 = mod[2:3], mod[3:4], mod[4:5], mod[5:6]

    lane = lax.broadcasted_iota(jnp.int32, (CHUNK, 2 * GM_GROUP_DIM), 1)
    first_half = lane < GM_GROUP_DIM
    vg = vg_ref[0]
    chunks = []
    for c in range(tm // CHUNK):
        cols = []
        for gp in range(GM_GROUPS // 2):
            rhs = vg[c * CHUNK:(c + 1) * CHUNK, gp * 128:(gp + 1) * 128]
            s0 = jnp.dot(ws_ref[2 * gp], rhs, preferred_element_type=F32)
            s1 = jnp.dot(ws_ref[2 * gp + 1], rhs, preferred_element_type=F32)
            cols.append(jnp.where(first_half, s0, s1))
        chunks.append(jnp.concatenate(cols, axis=1) + bs_ref[...])
    s = jnp.concatenate(chunks, axis=0)
    gm = (u_ref[0].astype(F32) * s).astype(BF16)

    y = (ga_ref[0].astype(F32) * jnp.dot(attn_ref[0], wa_ref[...], preferred_element_type=F32)
         + gb_ref[0].astype(F32) * jnp.dot(gm, wg_ref[...], preferred_element_type=F32))
    x1 = x_ref[0] + gate1 * jnp.dot(y.astype(BF16), wo_ref[...], preferred_element_type=F32)

    h2 = _modulate(x1, g2_ref[...], shift2, scale2).astype(BF16)
    f = jnp.maximum(jnp.dot(h2, w1_ref[...], preferred_element_type=F32), 0.0)
    f = (f * f).astype(BF16)
    o_ref[0] = x1 + gate2 * jnp.dot(f, w2_ref[...], preferred_element_type=F32)


def _post(x, mods3, attn, u, vg, ga, gb, gm_ws, bs_full, w_br_attn, w_br_gm, w_out, norm2_g, w_ff1, w_ff2):
    b, n, d = x.shape
    tm = POST_TILE
    tok = lambda w: pl.BlockSpec((1, tm, w), lambda i, t: (i, t, 0))
    return pl.pallas_call(
        _post_kernel,
        grid=(b, n // tm),
        in_specs=[
            tok(d),
            pl.BlockSpec((1, N_MOD, d), lambda i, t: (i, 0, 0)),
            tok(ATTN_WIDTH), tok(GM_WIDTH), tok(GM_WIDTH), tok(d), tok(d),
            _const_spec((GM_GROUPS, CHUNK, CHUNK)),
            _const_spec((CHUNK, GM_WIDTH)),
            _const_spec((ATTN_WIDTH, d)),
            _const_spec((GM_WIDTH, d)),
            _const_spec((d, d)),
            _const_spec((1, d)),
            _const_spec((d, D_FF)),
            _const_spec((D_FF, d)),
        ],
        out_specs=tok(d),
        out_shape=jax.ShapeDtypeStruct((b, n, d), F32),
        compiler_params=pltpu.CompilerParams(
            dimension_semantics=("arbitrary", "arbitrary"), vmem_limit_bytes=V7X_VMEM_LIMIT_BYTES),
        name="post",
    )(x, mods3, attn, u, vg, ga, gb, gm_ws, bs_full, w_br_attn, w_br_gm, w_out, norm2_g, w_ff1, w_ff2)


def _rope_tables_t(n_tokens, n_ctx):
    rows = n_tokens // GRID_W
    row = jnp.repeat(jnp.arange(rows, dtype=F32), GRID_W)
    col = jnp.tile(jnp.arange(GRID_W, dtype=F32), rows)
    inv = ROPE_THETA ** (-jnp.arange(0, ROT_AXIS_DIM, 2, dtype=F32) / ROT_AXIS_DIM)
    ang_t = jnp.concatenate([inv[:, None] * row[None, :], inv[:, None] * col[None, :]], axis=0)
    cos_t = jnp.concatenate([jnp.ones((ROT_AXIS_DIM, n_ctx), F32), jnp.cos(ang_t)], axis=1)
    sin_t = jnp.concatenate([jnp.zeros((ROT_AXIS_DIM, n_ctx), F32), jnp.sin(ang_t)], axis=1)
    return cos_t, sin_t


def kernel(x, c, ctx, c_ctx, w_mod, b_mod, norm1_g, norm2_g, w_in, q_norm_g, k_norm_g, gm_norm_g,
           gm_ws, gm_bs, w_br_attn, w_br_gm, w_out, w_ff1, w_ff2):
    b, n, d = x.shape
    assert w_mod.shape[0] == 1, "single-layer block"
    assert (b, d) == c.shape and d == D_MODEL and n % max(TOK_TILE, Q_TILE) == 0

    pad_rows = (-(b + 1)) % 8
    c_all = jnp.concatenate([c, c_ctx[None, :], jnp.zeros((pad_rows, d), c.dtype)], axis=0)
    mods3 = _adaln(c_all, w_mod[0], b_mod[0]).reshape(b + 1 + pad_rows, N_MOD, d)

    w_in_b = w_in[0].astype(BF16)
    k_norm_col = k_norm_g[0].reshape(HEAD_DIM, 1)
    q_norm_col = q_norm_g[0].reshape(HEAD_DIM, 1)
    gm_norm_row = gm_norm_g[0].reshape(1, GM_WIDTH)
    gid = jnp.arange(GM_WIDTH) // GM_GROUP_DIM
    group_ones = (gid[:, None] == gid[None, :]).astype(BF16)
    cos_t, sin_t = _rope_tables_t(n, ctx.shape[1])

    k, vt, qt, u, vg, ga, gb = _in_proj(
        x, ctx, mods3, norm1_g, w_in_b, k_norm_col, q_norm_col, gm_norm_row, group_ones, cos_t, sin_t)
    attn = _attention(qt, k, vt)

    bs_full = jnp.repeat(gm_bs[0].T, GM_GROUP_DIM, axis=1)
    return _post(x, mods3, attn, u, vg, ga, gb, gm_ws[0].astype(BF16), bs_full,
                 w_br_attn[0].astype(BF16), w_br_gm[0].astype(BF16), w_out[0].astype(BF16),
                 norm2_g, w_ff1[0].astype(BF16), w_ff2[0].astype(BF16))
```

```python
import functools
import math

import jax
import jax.numpy as jnp
from jax import lax
from jax.experimental import pallas as pl
from jax.experimental.pallas import tpu as pltpu

F32 = jnp.float32
BF16 = jnp.bfloat16

D_MODEL = 1024
GRID_W = 64
HEAD_DIM = 64
N_Q_HEADS = 8
N_KV_HEADS = 2
Q_PER_KV = N_Q_HEADS // N_KV_HEADS
ATTN_WIDTH = N_Q_HEADS * HEAD_DIM
GM_GROUPS = 8
GM_GROUP_DIM = 64
GM_WIDTH = GM_GROUPS * GM_GROUP_DIM
CHUNK = 128
D_FF = 4 * D_MODEL
ROPE_THETA = 10000.0
ROT_AXIS_DIM = HEAD_DIM // 2
EPS = 1e-6
N_MOD = 6

K_W = N_KV_HEADS * HEAD_DIM
KV_COLS = 2 * K_W
Q_OFF = KV_COLS
U_OFF = Q_OFF + ATTN_WIDTH
VG_OFF = U_OFF + GM_WIDTH
GA_OFF = VG_OFF + GM_WIDTH
GB_OFF = GA_OFF + D_MODEL
D_IN = GB_OFF + D_MODEL

V7X_VMEM_LIMIT_BYTES = 56 * 1024 * 1024

TOK_TILE = 256
POST_TILE = 512
Q_TILE = 256
KV_TILE = 256
N_SCORE_BUFS = 4
STEPS_PER_TRIP = 12
ONES_ROWS = 16

Q_SCALE_LOG2E = (HEAD_DIM ** -0.5) * math.log2(math.e)


def _sigmoid(x):
    return 1.0 / (1.0 + jnp.exp(-x))


def _gelu_tanh(x):
    c = math.sqrt(2.0 / math.pi)
    return 0.5 * x * (1.0 + jnp.tanh(c * (x + 0.044715 * (x * x * x))))


def _modulate(x, g, shift, scale):
    ms = jnp.mean(x * x, axis=-1, keepdims=True)
    return (x * lax.rsqrt(ms + EPS) * g) * (1.0 + scale) + shift


def _const_spec(shape):
    return pl.BlockSpec(shape, lambda *_: (0,) * len(shape), pipeline_mode=pl.Buffered(1))


def _adaln_kernel(c_ref, w_ref, b_ref, o_ref):
    c = c_ref[...]
    a = (c * _sigmoid(c)).astype(BF16)
    o_ref[...] = jnp.dot(a, w_ref[...].astype(BF16), preferred_element_type=F32) + b_ref[...]


def _adaln(c_all, w_mod, b_mod):
    rows, d = c_all.shape
    n = w_mod.shape[1]
    tn = 1024
    return pl.pallas_call(
        _adaln_kernel,
        grid=(n // tn,),
        in_specs=[
            pl.BlockSpec((rows, d), lambda j: (0, 0)),
            pl.BlockSpec((d, tn), lambda j: (0, j)),
            pl.BlockSpec((1, tn), lambda j: (0, j)),
        ],
        out_specs=pl.BlockSpec((rows, tn), lambda j: (0, j)),
        out_shape=jax.ShapeDtypeStruct((rows, n), F32),
        compiler_params=pltpu.CompilerParams(
            dimension_semantics=("arbitrary",), vmem_limit_bytes=V7X_VMEM_LIMIT_BYTES),
        name="adaln",
    )(c_all, w_mod, b_mod.reshape(1, n))


def _head_norm_t(blk, g_col):
    ms = jnp.mean(blk * blk, axis=0, keepdims=True)
    return blk * lax.rsqrt(ms + EPS) * g_col


def _rope_t(y, cos_t, sin_t):
    y1, y2 = y[:ROT_AXIS_DIM], y[ROT_AXIS_DIM:]
    return jnp.concatenate([y1 * cos_t - y2 * sin_t, y2 * cos_t + y1 * sin_t], axis=0)


def _in_proj_kernel(x_next_ref, ctx_ref, mod_ctx_ref, mod_ref, g1_ref, w_ref, kg_ref, qg_ref, gmg_ref,
                    cos_ref, sin_ref, k_ref, vt_ref, qt_ref, u_ref, vg_ref, ga_ref, gb_ref, h_ref):
    @pl.when(pl.program_id(1) == 0)
    def _():
        mod_ctx = mod_ctx_ref[0]
        h_ref[...] = _modulate(ctx_ref[0], g1_ref[...], mod_ctx[0:1], mod_ctx[1:2]).astype(BF16)

    p = jnp.dot(h_ref[...], w_ref[...], preferred_element_type=F32)

    cos_t = cos_ref[...]
    sin_t = sin_ref[...]
    pt = p[:, :U_OFF].T
    kg = kg_ref[...]
    kt = jnp.concatenate(
        [_rope_t(_head_norm_t(pt[HEAD_DIM * j:HEAD_DIM * (j + 1)], kg), cos_t, sin_t)
         for j in range(N_KV_HEADS)], axis=0)
    k_ref[0] = kt.T.astype(BF16)
    for j in range(N_KV_HEADS):
        vt_ref[0, j] = pt[K_W + HEAD_DIM * j:K_W + HEAD_DIM * (j + 1)].astype(BF16)
    qg = qg_ref[...]
    for j in range(N_Q_HEADS):
        blk = pt[Q_OFF + HEAD_DIM * j:Q_OFF + HEAD_DIM * (j + 1)]
        qt_ref[0, j] = (_rope_t(_head_norm_t(blk, qg), cos_t, sin_t) * Q_SCALE_LOG2E).astype(BF16)

    u_ref[0] = _gelu_tanh(p[:, U_OFF:VG_OFF]).astype(BF16)
    vgt = _gelu_tanh(p[:, VG_OFF:GA_OFF]).T
    gmg = gmg_ref[...]
    vgn_t = jnp.concatenate(
        [_head_norm_t(vgt[GM_GROUP_DIM * g:GM_GROUP_DIM * (g + 1)], gmg[GM_GROUP_DIM * g:GM_GROUP_DIM * (g + 1)])
         for g in range(GM_GROUPS)], axis=0)
    vg_ref[0] = vgn_t.T.astype(BF16)
    ga_ref[0] = p[:, GA_OFF:GB_OFF].astype(BF16)
    gb_ref[0] = p[:, GB_OFF:D_IN].astype(BF16)

    mod = mod_ref[0]
    h_ref[...] = _modulate(x_next_ref[0], g1_ref[...], mod[0:1], mod[1:2]).astype(BF16)


def _in_proj(x, ctx, mods3, norm1_g, w_in, k_norm_col, q_norm_col, gm_norm_col, cos_t, sin_t):
    b, n, d = x.shape
    tm = TOK_TILE
    assert ctx.shape == (b, tm, d)
    n_keys = n + tm
    lat = lambda t: jnp.maximum(t - 1, 0)
    tok = lambda w: pl.BlockSpec((1, tm, w), lambda i, t: (i, lat(t), 0))
    return pl.pallas_call(
        _in_proj_kernel,
        grid=(b, n // tm + 1),
        in_specs=[
            pl.BlockSpec((1, tm, d), lambda i, t: (i, jnp.minimum(t, n // tm - 1), 0)),
            pl.BlockSpec((1, tm, d), lambda i, t: (i, 0, 0)),
            pl.BlockSpec((1, N_MOD, d), lambda i, t: (b, 0, 0)),
            pl.BlockSpec((1, N_MOD, d), lambda i, t: (i, 0, 0)),
            _const_spec((1, d)),
            _const_spec((d, D_IN)),
            _const_spec((HEAD_DIM, 1)),
            _const_spec((HEAD_DIM, 1)),
            _const_spec((GM_WIDTH, 1)),
            pl.BlockSpec((ROT_AXIS_DIM, tm), lambda i, t: (0, t)),
            pl.BlockSpec((ROT_AXIS_DIM, tm), lambda i, t: (0, t)),
        ],
        out_specs=[
            pl.BlockSpec((1, tm, K_W), lambda i, t: (i, t, 0)),
            pl.BlockSpec((1, N_KV_HEADS, HEAD_DIM, tm), lambda i, t: (i, 0, 0, t)),
            pl.BlockSpec((1, N_Q_HEADS, HEAD_DIM, tm), lambda i, t: (i, 0, 0, lat(t))),
            tok(GM_WIDTH), tok(GM_WIDTH), tok(d), tok(d),
        ],
        out_shape=[
            jax.ShapeDtypeStruct((b, n_keys, K_W), BF16),
            jax.ShapeDtypeStruct((b, N_KV_HEADS, HEAD_DIM, n_keys), BF16),
            jax.ShapeDtypeStruct((b, N_Q_HEADS, HEAD_DIM, n), BF16),
            jax.ShapeDtypeStruct((b, n, GM_WIDTH), BF16),
            jax.ShapeDtypeStruct((b, n, GM_WIDTH), BF16),
            jax.ShapeDtypeStruct((b, n, d), BF16),
            jax.ShapeDtypeStruct((b, n, d), BF16),
        ],
        scratch_shapes=[pltpu.VMEM((tm, d), BF16)],
        compiler_params=pltpu.CompilerParams(
            dimension_semantics=("arbitrary", "arbitrary"), vmem_limit_bytes=V7X_VMEM_LIMIT_BYTES),
        name="in_proj",
    )(x, ctx, mods3, mods3, norm1_g, w_in, k_norm_col, q_norm_col, gm_norm_col, cos_t, sin_t)


def _attention_kernel(qt_ref, k_ref, vt_ref, o_ref, qz_ref, m_ref, acc_ref, *s_refs):
    def one_kv_head(kv_head, carry):
        _attention_kv_head(kv_head, qt_ref, k_ref, vt_ref, o_ref, qz_ref, m_ref, acc_ref, s_refs)
        return carry

    lax.fori_loop(0, N_KV_HEADS, one_kv_head, 0)


def _attention_kv_head(kv_head, qt_ref, k_ref, vt_ref, o_ref, qz_ref, m_ref, acc_ref, s_refs):
    tq = qt_ref.shape[3]
    n_tiles = k_ref.shape[1] // KV_TILE

    q_cat = jnp.concatenate([qt_ref[0, Q_PER_KV * kv_head + g] for g in range(Q_PER_KV)], axis=1)
    zeros = jnp.zeros_like(q_cat)
    qz_ref[...] = jnp.concatenate(
        [jnp.where(kv_head == 0, q_cat, zeros), jnp.where(kv_head == 1, q_cat, zeros)], axis=0)

    def tile_slice(i):
        return pl.ds(i * KV_TILE if isinstance(i, int) else pl.multiple_of(i * KV_TILE, KV_TILE), KV_TILE)

    def produce(i, buf):
        s = jnp.dot(k_ref[0, tile_slice(i), :], qz_ref[...], preferred_element_type=F32)
        s_refs[buf][...] = s
        return jnp.max(s, axis=0, keepdims=True)

    def consume(i, buf, s_max):
        vt = vt_ref[0, kv_head, :, tile_slice(i)]
        v1t = jnp.concatenate([vt, jnp.ones((ONES_ROWS, KV_TILE), vt.dtype)], axis=0)
        m_old = m_ref[...]
        m_new = jnp.maximum(m_old, s_max)
        p = jnp.exp2(s_refs[buf][...] - m_new).astype(BF16)
        pv = jnp.dot(v1t, p, preferred_element_type=F32)
        acc_ref[...] = jnp.exp2(m_old - m_new) * acc_ref[...] + pv
        m_ref[...] = m_new

    m_ref[...] = jnp.full(m_ref.shape, -jnp.inf, F32)
    acc_ref[...] = jnp.zeros(acc_ref.shape, F32)

    def steps(first_tile, count, maxes):
        max_0, max_1 = maxes
        for j in range(count):
            max_2 = produce(first_tile + j + 2, (j + 2) % N_SCORE_BUFS)
            consume(first_tile + j, j % N_SCORE_BUFS, max_0)
            max_0, max_1 = max_1, max_2
        return max_0, max_1

    assert STEPS_PER_TRIP % N_SCORE_BUFS == 0
    maxes = (produce(0, 0), produce(1, 1))
    n_steps = n_tiles - 2
    n_trips = n_steps // STEPS_PER_TRIP
    maxes = lax.fori_loop(0, n_trips, lambda t, mx: steps(t * STEPS_PER_TRIP, STEPS_PER_TRIP, mx), maxes)
    done = n_trips * STEPS_PER_TRIP
    maxes = steps(done, n_steps - done, maxes)
    for j in range(2):
        consume(n_steps + j, (n_steps + j) % N_SCORE_BUFS, maxes[j])

    acc = acc_ref[...]
    o_t = acc[:HEAD_DIM] / acc[HEAD_DIM:HEAD_DIM + 1]
    pairs = []
    for g in range(0, Q_PER_KV, 2):
        two = jnp.concatenate([o_t[:, g * tq:(g + 1) * tq], o_t[:, (g + 1) * tq:(g + 2) * tq]], axis=0)
        pairs.append(two.T)
    width = Q_PER_KV * HEAD_DIM
    o_ref[0, :, pl.ds(pl.multiple_of(kv_head * width, width), width)] = (
        jnp.concatenate(pairs, axis=1).astype(o_ref.dtype))


def _attention(qt, k, vt):
    b, _, _, n = qt.shape
    n_keys = k.shape[1]
    assert n_keys % KV_TILE == 0
    tq = Q_TILE
    cols = Q_PER_KV * tq
    return pl.pallas_call(
        _attention_kernel,
        grid=(b, n // tq),
        in_specs=[
            pl.BlockSpec((1, N_Q_HEADS, HEAD_DIM, tq), lambda i, t: (i, 0, 0, t)),
            pl.BlockSpec((1, n_keys, K_W), lambda i, t: (i, 0, 0)),
            pl.BlockSpec((1, N_KV_HEADS, HEAD_DIM, n_keys), lambda i, t: (i, 0, 0, 0)),
        ],
        out_specs=pl.BlockSpec((1, tq, ATTN_WIDTH), lambda i, t: (i, t, 0)),
        out_shape=jax.ShapeDtypeStruct((b, n, ATTN_WIDTH), BF16),
        scratch_shapes=[
            pltpu.VMEM((K_W, cols), BF16),
            pltpu.VMEM((1, cols), F32),
            pltpu.VMEM((HEAD_DIM + ONES_ROWS, cols), F32),
        ] + [pltpu.VMEM((KV_TILE, cols), F32)] * N_SCORE_BUFS,
        compiler_params=pltpu.CompilerParams(
            dimension_semantics=("arbitrary", "arbitrary"),
            vmem_limit_bytes=V7X_VMEM_LIMIT_BYTES),
        name="attention",
    )(qt, k, vt)


def _post_kernel(x_ref, mod_ref, attn_ref, u_ref, vg_ref, ga_ref, gb_ref, ws_ref, bs_ref,
                 wa_ref, wg_ref, wo_ref, g2_ref, w1_ref, w2_ref, o_ref):
    tm = x_ref.shape[1]
    mod = mod_ref[0]
    gate1, shift2, scale2, gate2 = mod[2:3], mod[3:4], mod[4:5], mod[5:6]

    lane = lax.broadcasted_iota(jnp.int32, (CHUNK, 2 * GM_GROUP_DIM), 1)
    first_half = lane < GM_GROUP_DIM
    vg = vg_ref[0]
    chunks = []
    for c in range(tm // CHUNK):
        cols = []
        for gp in range(GM_GROUPS // 2):
            rhs = vg[c * CHUNK:(c + 1) * CHUNK, gp * 128:(gp + 1) * 128]
            s0 = jnp.dot(ws_ref[2 * gp], rhs, preferred_element_type=F32)
            s1 = jnp.dot(ws_ref[2 * gp + 1], rhs, preferred_element_type=F32)
            cols.append(jnp.where(first_half, s0, s1))
        chunks.append(jnp.concatenate(cols, axis=1) + bs_ref[...])
    s = jnp.concatenate(chunks, axis=0)
    gm = (u_ref[0].astype(F32) * s).astype(BF16)

    y = (_sigmoid(ga_ref[0].astype(F32)) * jnp.dot(attn_ref[0], wa_ref[...], preferred_element_type=F32)
         + _sigmoid(gb_ref[0].astype(F32)) * jnp.dot(gm, wg_ref[...], preferred_element_type=F32))
    x1 = x_ref[0] + gate1 * jnp.dot(y.astype(BF16), wo_ref[...], preferred_element_type=F32)

    h2 = _modulate(x1, g2_ref[...], shift2, scale2).astype(BF16)
    f = jnp.maximum(jnp.dot(h2, w1_ref[...], preferred_element_type=F32), 0.0)
    f = (f * f).astype(BF16)
    o_ref[0] = x1 + gate2 * jnp.dot(f, w2_ref[...], preferred_element_type=F32)


def _post(x, mods3, attn, u, vg, ga, gb, gm_ws, bs_full, w_br_attn, w_br_gm, w_out, norm2_g, w_ff1, w_ff2):
    b, n, d = x.shape
    tm = POST_TILE
    tok = lambda w: pl.BlockSpec((1, tm, w), lambda i, t: (i, t, 0))
    return pl.pallas_call(
        _post_kernel,
        grid=(b, n // tm),
        in_specs=[
            tok(d),
            pl.BlockSpec((1, N_MOD, d), lambda i, t: (i, 0, 0)),
            tok(ATTN_WIDTH), tok(GM_WIDTH), tok(GM_WIDTH), tok(d), tok(d),
            _const_spec((GM_GROUPS, CHUNK, CHUNK)),
            _const_spec((CHUNK, GM_WIDTH)),
            _const_spec((ATTN_WIDTH, d)),
            _const_spec((GM_WIDTH, d)),
            _const_spec((d, d)),
            _const_spec((1, d)),
            _const_spec((d, D_FF)),
            _const_spec((D_FF, d)),
        ],
        out_specs=tok(d),
        out_shape=jax.ShapeDtypeStruct((b, n, d), F32),
        compiler_params=pltpu.CompilerParams(
            dimension_semantics=("arbitrary", "arbitrary"), vmem_limit_bytes=V7X_VMEM_LIMIT_BYTES),
        name="post",
    )(x, mods3, attn, u, vg, ga, gb, gm_ws, bs_full, w_br_attn, w_br_gm, w_out, norm2_g, w_ff1, w_ff2)


def _rope_tables_t(n_tokens, n_ctx):
    rows = n_tokens // GRID_W
    row = jnp.repeat(jnp.arange(rows, dtype=F32), GRID_W)
    col = jnp.tile(jnp.arange(GRID_W, dtype=F32), rows)
    inv = ROPE_THETA ** (-jnp.arange(0, ROT_AXIS_DIM, 2, dtype=F32) / ROT_AXIS_DIM)
    ang_t = jnp.concatenate([inv[:, None] * row[None, :], inv[:, None] * col[None, :]], axis=0)
    cos_t = jnp.concatenate([jnp.ones((ROT_AXIS_DIM, n_ctx), F32), jnp.cos(ang_t)], axis=1)
    sin_t = jnp.concatenate([jnp.zeros((ROT_AXIS_DIM, n_ctx), F32), jnp.sin(ang_t)], axis=1)
    return cos_t, sin_t


def kernel(x, c, ctx, c_ctx, w_mod, b_mod, norm1_g, norm2_g, w_in, q_norm_g, k_norm_g, gm_norm_g,
           gm_ws, gm_bs, w_br_attn, w_br_gm, w_out, w_ff1, w_ff2):
    b, n, d = x.shape
    assert w_mod.shape[0] == 1, "single-layer block"
    assert (b, d) == c.shape and d == D_MODEL and n % max(TOK_TILE, Q_TILE) == 0

    pad_rows = (-(b + 1)) % 8
    c_all = jnp.concatenate([c, c_ctx[None, :], jnp.zeros((pad_rows, d), c.dtype)], axis=0)
    mods3 = _adaln(c_all, w_mod[0], b_mod[0]).reshape(b + 1 + pad_rows, N_MOD, d)

    w_in_b = w_in[0].astype(BF16)
    k_norm_col = k_norm_g[0].reshape(HEAD_DIM, 1)
    q_norm_col = q_norm_g[0].reshape(HEAD_DIM, 1)
    gm_norm_col = gm_norm_g[0].reshape(GM_WIDTH, 1)
    cos_t, sin_t = _rope_tables_t(n, ctx.shape[1])

    k, vt, qt, u, vg, ga, gb = _in_proj(
        x, ctx, mods3, norm1_g, w_in_b, k_norm_col, q_norm_col, gm_norm_col, cos_t, sin_t)
    attn = _attention(qt, k, vt)

    bs_full = jnp.repeat(gm_bs[0].T, GM_GROUP_DIM, axis=1)
    return _post(x, mods3, attn, u, vg, ga, gb, gm_ws[0].astype(BF16), bs_full,
                 w_br_attn[0].astype(BF16), w_br_gm[0].astype(BF16), w_out[0].astype(BF16),
                 norm2_g, w_ff1[0].astype(BF16), w_ff2[0].astype(BF16))
```

```python
import functools
import math

import jax
import jax.numpy as jnp
from jax import lax
from jax.experimental import pallas as pl
from jax.experimental.pallas import tpu as pltpu

F32 = jnp.float32
BF16 = jnp.bfloat16

D_MODEL = 1024
GRID_W = 64
HEAD_DIM = 64
N_Q_HEADS = 8
N_KV_HEADS = 2
Q_PER_KV = N_Q_HEADS // N_KV_HEADS
ATTN_WIDTH = N_Q_HEADS * HEAD_DIM
GM_GROUPS = 8
GM_GROUP_DIM = 64
GM_WIDTH = GM_GROUPS * GM_GROUP_DIM
CHUNK = 128
D_FF = 4 * D_MODEL
ROPE_THETA = 10000.0
ROT_AXIS_DIM = HEAD_DIM // 2
EPS = 1e-6
N_MOD = 6

K_W = N_KV_HEADS * HEAD_DIM
KV_COLS = 2 * K_W
Q_OFF = KV_COLS
U_OFF = Q_OFF + ATTN_WIDTH
VG_OFF = U_OFF + GM_WIDTH
GA_OFF = VG_OFF + GM_WIDTH
GB_OFF = GA_OFF + D_MODEL
D_IN = GB_OFF + D_MODEL

V7X_VMEM_LIMIT_BYTES = 56 * 1024 * 1024

TOK_TILE = 256
POST_TILE = 512
Q_TILE = 256
KV_TILE = 256
N_SCORE_BUFS = 4
STEPS_PER_TRIP = 12
COL_HALF = 512
ONES_ROWS = 16

Q_SCALE_LOG2E = (HEAD_DIM ** -0.5) * math.log2(math.e)


def _sigmoid(x):
    return 1.0 / (1.0 + jnp.exp(-x))


def _gelu_tanh(x):
    c = math.sqrt(2.0 / math.pi)
    return 0.5 * x * (1.0 + jnp.tanh(c * (x + 0.044715 * (x * x * x))))


def _modulate(x, g, shift, scale):
    ms = jnp.mean(x * x, axis=-1, keepdims=True)
    return (x * lax.rsqrt(ms + EPS) * g) * (1.0 + scale) + shift


def _const_spec(shape):
    return pl.BlockSpec(shape, lambda *_: (0,) * len(shape), pipeline_mode=pl.Buffered(1))


def _adaln_kernel(c_ref, w_ref, b_ref, o_ref):
    c = c_ref[...]
    a = (c * _sigmoid(c)).astype(BF16)
    o_ref[...] = jnp.dot(a, w_ref[...].astype(BF16), preferred_element_type=F32) + b_ref[...]


def _adaln(c_all, w_mod, b_mod):
    rows, d = c_all.shape
    n = w_mod.shape[1]
    tn = 1024
    return pl.pallas_call(
        _adaln_kernel,
        grid=(n // tn,),
        in_specs=[
            pl.BlockSpec((rows, d), lambda j: (0, 0)),
            pl.BlockSpec((d, tn), lambda j: (0, j)),
            pl.BlockSpec((1, tn), lambda j: (0, j)),
        ],
        out_specs=pl.BlockSpec((rows, tn), lambda j: (0, j)),
        out_shape=jax.ShapeDtypeStruct((rows, n), F32),
        compiler_params=pltpu.CompilerParams(
            dimension_semantics=("arbitrary",), vmem_limit_bytes=V7X_VMEM_LIMIT_BYTES),
        name="adaln",
    )(c_all, w_mod, b_mod.reshape(1, n))


def _head_norm_t(blk, g_col):
    ms = jnp.mean(blk * blk, axis=0, keepdims=True)
    return blk * lax.rsqrt(ms + EPS) * g_col


def _rope_t(y, cos_t, sin_t):
    y1, y2 = y[:ROT_AXIS_DIM], y[ROT_AXIS_DIM:]
    return jnp.concatenate([y1 * cos_t - y2 * sin_t, y2 * cos_t + y1 * sin_t], axis=0)


def _in_proj_kernel(x_next_ref, ctx_ref, mod_ctx_ref, mod_ref, g1_ref, w_ref, kg_ref, qg_ref, gmg_ref,
                    cos_ref, sin_ref, k_ref, vt_ref, qt_ref, u_ref, vg_ref, ga_ref, gb_ref, h_ref):
    @pl.when(pl.program_id(1) == 0)
    def _():
        mod_ctx = mod_ctx_ref[0]
        h_ref[...] = _modulate(ctx_ref[0], g1_ref[...], mod_ctx[0:1], mod_ctx[1:2]).astype(BF16)

    p = jnp.dot(h_ref[...], w_ref[...], preferred_element_type=F32)

    cos_t = cos_ref[...]
    sin_t = sin_ref[...]
    pt = p[:, :U_OFF].T
    kg = kg_ref[...]
    kt = jnp.concatenate(
        [_rope_t(_head_norm_t(pt[HEAD_DIM * j:HEAD_DIM * (j + 1)], kg), cos_t, sin_t)
         for j in range(N_KV_HEADS)], axis=0)
    k_ref[0] = kt.T.astype(BF16)
    for j in range(N_KV_HEADS):
        vt_ref[0, j] = pt[K_W + HEAD_DIM * j:K_W + HEAD_DIM * (j + 1)].astype(BF16)
    qg = qg_ref[...]
    for j in range(N_Q_HEADS):
        blk = pt[Q_OFF + HEAD_DIM * j:Q_OFF + HEAD_DIM * (j + 1)]
        qt_ref[0, j] = (_rope_t(_head_norm_t(blk, qg), cos_t, sin_t) * Q_SCALE_LOG2E).astype(BF16)

    u_ref[0] = _gelu_tanh(p[:, U_OFF:VG_OFF]).astype(BF16)
    vgt = _gelu_tanh(p[:, VG_OFF:GA_OFF]).T
    gmg = gmg_ref[...]
    vgn_t = jnp.concatenate(
        [_head_norm_t(vgt[GM_GROUP_DIM * g:GM_GROUP_DIM * (g + 1)], gmg[GM_GROUP_DIM * g:GM_GROUP_DIM * (g + 1)])
         for g in range(GM_GROUPS)], axis=0)
    vg_ref[0] = vgn_t.T.astype(BF16)
    ga_ref[0] = p[:, GA_OFF:GB_OFF].astype(BF16)
    gb_ref[0] = p[:, GB_OFF:D_IN].astype(BF16)

    mod = mod_ref[0]
    h_ref[...] = _modulate(x_next_ref[0], g1_ref[...], mod[0:1], mod[1:2]).astype(BF16)


def _in_proj(x, ctx, mods3, norm1_g, w_in, k_norm_col, q_norm_col, gm_norm_col, cos_t, sin_t):
    b, n, d = x.shape
    tm = TOK_TILE
    assert ctx.shape == (b, tm, d)
    n_keys = n + tm
    lat = lambda t: jnp.maximum(t - 1, 0)
    tok = lambda w: pl.BlockSpec((1, tm, w), lambda i, t: (i, lat(t), 0))
    return pl.pallas_call(
        _in_proj_kernel,
        grid=(b, n // tm + 1),
        in_specs=[
            pl.BlockSpec((1, tm, d), lambda i, t: (i, jnp.minimum(t, n // tm - 1), 0)),
            pl.BlockSpec((1, tm, d), lambda i, t: (i, 0, 0)),
            pl.BlockSpec((1, N_MOD, d), lambda i, t: (b, 0, 0)),
            pl.BlockSpec((1, N_MOD, d), lambda i, t: (i, 0, 0)),
            _const_spec((1, d)),
            _const_spec((d, D_IN)),
            _const_spec((HEAD_DIM, 1)),
            _const_spec((HEAD_DIM, 1)),
            _const_spec((GM_WIDTH, 1)),
            pl.BlockSpec((ROT_AXIS_DIM, tm), lambda i, t: (0, t)),
            pl.BlockSpec((ROT_AXIS_DIM, tm), lambda i, t: (0, t)),
        ],
        out_specs=[
            pl.BlockSpec((1, tm, K_W), lambda i, t: (i, t, 0)),
            pl.BlockSpec((1, N_KV_HEADS, HEAD_DIM, tm), lambda i, t: (i, 0, 0, t)),
            pl.BlockSpec((1, N_Q_HEADS, HEAD_DIM, tm), lambda i, t: (i, 0, 0, lat(t))),
            tok(GM_WIDTH), tok(GM_WIDTH), tok(d), tok(d),
        ],
        out_shape=[
            jax.ShapeDtypeStruct((b, n_keys, K_W), BF16),
            jax.ShapeDtypeStruct((b, N_KV_HEADS, HEAD_DIM, n_keys), BF16),
            jax.ShapeDtypeStruct((b, N_Q_HEADS, HEAD_DIM, n), BF16),
            jax.ShapeDtypeStruct((b, n, GM_WIDTH), BF16),
            jax.ShapeDtypeStruct((b, n, GM_WIDTH), BF16),
            jax.ShapeDtypeStruct((b, n, d), BF16),
            jax.ShapeDtypeStruct((b, n, d), BF16),
        ],
        scratch_shapes=[pltpu.VMEM((tm, d), BF16)],
        compiler_params=pltpu.CompilerParams(
            dimension_semantics=("arbitrary", "arbitrary"), vmem_limit_bytes=V7X_VMEM_LIMIT_BYTES),
        name="in_proj",
    )(x, ctx, mods3, mods3, norm1_g, w_in, k_norm_col, q_norm_col, gm_norm_col, cos_t, sin_t)


def _attention_kernel(qt_ref, k_ref, vt_ref, o_ref, qz_ref, m_ref, acc_ref, *s_refs):
    def one_kv_head(kv_head, carry):
        _attention_kv_head(kv_head, qt_ref, k_ref, vt_ref, o_ref, qz_ref, m_ref, acc_ref, s_refs)
        return carry

    lax.fori_loop(0, N_KV_HEADS, one_kv_head, 0)


def _attention_kv_head(kv_head, qt_ref, k_ref, vt_ref, o_ref, qz_ref, m_ref, acc_ref, s_refs):
    tq = qt_ref.shape[3]
    n_tiles = k_ref.shape[1] // KV_TILE

    q_cat = jnp.concatenate([qt_ref[0, Q_PER_KV * kv_head + g] for g in range(Q_PER_KV)], axis=1)
    zeros = jnp.zeros_like(q_cat)
    qz_ref[...] = jnp.concatenate(
        [jnp.where(kv_head == 0, q_cat, zeros), jnp.where(kv_head == 1, q_cat, zeros)], axis=0)

    def tile_slice(i):
        return pl.ds(i * KV_TILE if isinstance(i, int) else pl.multiple_of(i * KV_TILE, KV_TILE), KV_TILE)

    halves = [slice(h * COL_HALF, (h + 1) * COL_HALF) for h in range(m_ref.shape[1] // COL_HALF)]

    def step(i_prod, buf_prod, i_cons, buf_cons, s_max):
        if i_prod is not None:
            k_t = k_ref[0, tile_slice(i_prod), :]
        if i_cons is not None:
            vt = vt_ref[0, kv_head, :, tile_slice(i_cons)]
            v1t = jnp.concatenate([vt, jnp.ones((ONES_ROWS, KV_TILE), vt.dtype)], axis=0)
        new_max = []
        for h, cols in enumerate(halves):
            if i_prod is not None:
                s = jnp.dot(k_t, qz_ref[:, cols], preferred_element_type=F32)
                s_refs[buf_prod][:, cols] = s
                new_max.append(jnp.max(s, axis=0, keepdims=True))
            if i_cons is not None:
                m_old = m_ref[:, cols]
                m_new = jnp.maximum(m_old, s_max[h])
                m_ref[:, cols] = m_new
                p = jnp.exp2(s_refs[buf_cons][:, cols] - m_new).astype(BF16)
                acc_ref[:, cols] = jnp.exp2(m_old - m_new) * acc_ref[:, cols] + jnp.dot(
                    v1t, p, preferred_element_type=F32)
        return tuple(new_max)

    m_ref[...] = jnp.full(m_ref.shape, -jnp.inf, F32)
    acc_ref[...] = jnp.zeros(acc_ref.shape, F32)

    def steps(first_tile, count, maxes):
        max_0, max_1 = maxes
        for j in range(count):
            max_2 = step(first_tile + j + 2, (j + 2) % N_SCORE_BUFS, first_tile + j, j % N_SCORE_BUFS, max_0)
            max_0, max_1 = max_1, max_2
        return max_0, max_1

    assert STEPS_PER_TRIP % N_SCORE_BUFS == 0
    maxes = (step(0, 0, None, None, None), step(1, 1, None, None, None))
    n_steps = n_tiles - 2
    n_trips = n_steps // STEPS_PER_TRIP
    maxes = lax.fori_loop(0, n_trips, lambda t, mx: steps(t * STEPS_PER_TRIP, STEPS_PER_TRIP, mx), maxes)
    done = n_trips * STEPS_PER_TRIP
    maxes = steps(done, n_steps - done, maxes)
    for j in range(2):
        step(None, None, n_steps + j, (n_steps + j) % N_SCORE_BUFS, maxes[j])

    acc = acc_ref[...]
    o_t = acc[:HEAD_DIM] / acc[HEAD_DIM:HEAD_DIM + 1]
    pairs = []
    for g in range(0, Q_PER_KV, 2):
        two = jnp.concatenate([o_t[:, g * tq:(g + 1) * tq], o_t[:, (g + 1) * tq:(g + 2) * tq]], axis=0)
        pairs.append(two.T)
    width = Q_PER_KV * HEAD_DIM
    o_ref[0, :, pl.ds(pl.multiple_of(kv_head * width, width), width)] = (
        jnp.concatenate(pairs, axis=1).astype(o_ref.dtype))


def _attention(qt, k, vt):
    b, _, _, n = qt.shape
    n_keys = k.shape[1]
    assert n_keys % KV_TILE == 0
    tq = Q_TILE
    cols = Q_PER_KV * tq
    return pl.pallas_call(
        _attention_kernel,
        grid=(b, n // tq),
        in_specs=[
            pl.BlockSpec((1, N_Q_HEADS, HEAD_DIM, tq), lambda i, t: (i, 0, 0, t)),
            pl.BlockSpec((1, n_keys, K_W), lambda i, t: (i, 0, 0)),
            pl.BlockSpec((1, N_KV_HEADS, HEAD_DIM, n_keys), lambda i, t: (i, 0, 0, 0)),
        ],
        out_specs=pl.BlockSpec((1, tq, ATTN_WIDTH), lambda i, t: (i, t, 0)),
        out_shape=jax.ShapeDtypeStruct((b, n, ATTN_WIDTH), BF16),
        scratch_shapes=[
            pltpu.VMEM((K_W, cols), BF16),
            pltpu.VMEM((1, cols), F32),
            pltpu.VMEM((HEAD_DIM + ONES_ROWS, cols), F32),
        ] + [pltpu.VMEM((KV_TILE, cols), F32)] * N_SCORE_BUFS,
        compiler_params=pltpu.CompilerParams(
            dimension_semantics=("arbitrary", "arbitrary"),
            vmem_limit_bytes=V7X_VMEM_LIMIT_BYTES),
        name="attention",
    )(qt, k, vt)


def _post_kernel(x_ref, mod_ref, attn_ref, u_ref, vg_ref, ga_ref, gb_ref, ws_ref, bs_ref,
                 wa_ref, wg_ref, wo_ref, g2_ref, w1_ref, w2_ref, o_ref):
    tm = x_ref.shape[1]
    mod = mod_ref[0]
    gate1, shift2, scale2, gate2 = mod[2:3], mod[3:4], mod[4:5], mod[5:6]

    lane = lax.broadcasted_iota(jnp.int32, (CHUNK, 2 * GM_GROUP_DIM), 1)
    first_half = lane < GM_GROUP_DIM
    vg = vg_ref[0]
    chunks = []
    for c in range(tm // CHUNK):
        cols = []
        for gp in range(GM_GROUPS // 2):
            rhs = vg[c * CHUNK:(c + 1) * CHUNK, gp * 128:(gp + 1) * 128]
            s0 = jnp.dot(ws_ref[2 * gp], rhs, preferred_element_type=F32)
            s1 = jnp.dot(ws_ref[2 * gp + 1], rhs, preferred_element_type=F32)
            cols.append(jnp.where(first_half, s0, s1))
        chunks.append(jnp.concatenate(cols, axis=1) + bs_ref[...])
    s = jnp.concatenate(chunks, axis=0)
    gm = (u_ref[0].astype(F32) * s).astype(BF16)

    y = (_sigmoid(ga_ref[0].astype(F32)) * jnp.dot(attn_ref[0], wa_ref[...], preferred_element_type=F32)
         + _sigmoid(gb_ref[0].astype(F32)) * jnp.dot(gm, wg_ref[...], preferred_element_type=F32))
    x1 = x_ref[0] + gate1 * jnp.dot(y.astype(BF16), wo_ref[...], preferred_element_type=F32)

    h2 = _modulate(x1, g2_ref[...], shift2, scale2).astype(BF16)
    f = jnp.maximum(jnp.dot(h2, w1_ref[...], preferred_element_type=F32), 0.0)
    f = (f * f).astype(BF16)
    o_ref[0] = x1 + gate2 * jnp.dot(f, w2_ref[...], preferred_element_type=F32)


def _post(x, mods3, attn, u, vg, ga, gb, gm_ws, bs_full, w_br_attn, w_br_gm, w_out, norm2_g, w_ff1, w_ff2):
    b, n, d = x.shape
    tm = POST_TILE
    tok = lambda w: pl.BlockSpec((1, tm, w), lambda i, t: (i, t, 0))
    return pl.pallas_call(
        _post_kernel,
        grid=(b, n // tm),
        in_specs=[
            tok(d),
            pl.BlockSpec((1, N_MOD, d), lambda i, t: (i, 0, 0)),
            tok(ATTN_WIDTH), tok(GM_WIDTH), tok(GM_WIDTH), tok(d), tok(d),
            _const_spec((GM_GROUPS, CHUNK, CHUNK)),
            _const_spec((CHUNK, GM_WIDTH)),
            _const_spec((ATTN_WIDTH, d)),
            _const_spec((GM_WIDTH, d)),
            _const_spec((d, d)),
            _const_spec((1, d)),
            _const_spec((d, D_FF)),
            _const_spec((D_FF, d)),
        ],
        out_specs=tok(d),
        out_shape=jax.ShapeDtypeStruct((b, n, d), F32),
        compiler_params=pltpu.CompilerParams(
            dimension_semantics=("arbitrary", "arbitrary"), vmem_limit_bytes=V7X_VMEM_LIMIT_BYTES),
        name="post",
    )(x, mods3, attn, u, vg, ga, gb, gm_ws, bs_full, w_br_attn, w_br_gm, w_out, norm2_g, w_ff1, w_ff2)


def _rope_tables_t(n_tokens, n_ctx):
    rows = n_tokens // GRID_W
    row = jnp.repeat(jnp.arange(rows, dtype=F32), GRID_W)
    col = jnp.tile(jnp.arange(GRID_W, dtype=F32), rows)
    inv = ROPE_THETA ** (-jnp.arange(0, ROT_AXIS_DIM, 2, dtype=F32) / ROT_AXIS_DIM)
    ang_t = jnp.concatenate([inv[:, None] * row[None, :], inv[:, None] * col[None, :]], axis=0)
    cos_t = jnp.concatenate([jnp.ones((ROT_AXIS_DIM, n_ctx), F32), jnp.cos(ang_t)], axis=1)
    sin_t = jnp.concatenate([jnp.zeros((ROT_AXIS_DIM, n_ctx), F32), jnp.sin(ang_t)], axis=1)
    return cos_t, sin_t


def kernel(x, c, ctx, c_ctx, w_mod, b_mod, norm1_g, norm2_g, w_in, q_norm_g, k_norm_g, gm_norm_g,
           gm_ws, gm_bs, w_br_attn, w_br_gm, w_out, w_ff1, w_ff2):
    b, n, d = x.shape
    assert w_mod.shape[0] == 1, "single-layer block"
    assert (b, d) == c.shape and d == D_MODEL and n % max(TOK_TILE, Q_TILE) == 0

    pad_rows = (-(b + 1)) % 8
    c_all = jnp.concatenate([c, c_ctx[None, :], jnp.zeros((pad_rows, d), c.dtype)], axis=0)
    mods3 = _adaln(c_all, w_mod[0], b_mod[0]).reshape(b + 1 + pad_rows, N_MOD, d)

    w_in_b = w_in[0].astype(BF16)
    k_norm_col = k_norm_g[0].reshape(HEAD_DIM, 1)
    q_norm_col = q_norm_g[0].reshape(HEAD_DIM, 1)
    gm_norm_col = gm_norm_g[0].reshape(GM_WIDTH, 1)
    cos_t, sin_t = _rope_tables_t(n, ctx.shape[1])

    k, vt, qt, u, vg, ga, gb = _in_proj(
        x, ctx, mods3, norm1_g, w_in_b, k_norm_col, q_norm_col, gm_norm_col, cos_t, sin_t)
    attn = _attention(qt, k, vt)

    bs_full = jnp.repeat(gm_bs[0].T, GM_GROUP_DIM, axis=1)
    return _post(x, mods3, attn, u, vg, ga, gb, gm_ws[0].astype(BF16), bs_full,
                 w_br_attn[0].astype(BF16), w_br_gm[0].astype(BF16), w_out[0].astype(BF16),
                 norm2_g, w_ff1[0].astype(BF16), w_ff2[0].astype(BF16))
```

```python
import functools
import math

import jax
import jax.numpy as jnp
from jax import lax
from jax.experimental import pallas as pl
from jax.experimental.pallas import tpu as pltpu

F32 = jnp.float32
BF16 = jnp.bfloat16

D_MODEL = 1024
GRID_W = 64
HEAD_DIM = 64
N_Q_HEADS = 8
N_KV_HEADS = 2
Q_PER_KV = N_Q_HEADS // N_KV_HEADS
ATTN_WIDTH = N_Q_HEADS * HEAD_DIM
GM_GROUPS = 8
GM_GROUP_DIM = 64
GM_WIDTH = GM_GROUPS * GM_GROUP_DIM
CHUNK = 128
D_FF = 4 * D_MODEL
ROPE_THETA = 10000.0
ROT_AXIS_DIM = HEAD_DIM // 2
EPS = 1e-6
N_MOD = 6

K_W = N_KV_HEADS * HEAD_DIM
KV_COLS = 2 * K_W
Q_OFF = KV_COLS
U_OFF = Q_OFF + ATTN_WIDTH
VG_OFF = U_OFF + GM_WIDTH
GA_OFF = VG_OFF + GM_WIDTH
GB_OFF = GA_OFF + D_MODEL
D_IN = GB_OFF + D_MODEL

V7X_VMEM_LIMIT_BYTES = 56 * 1024 * 1024

TOK_TILE = 256
POST_TILE = 512
Q_TILE = 1024
KV_TILE = 256
N_SCORE_BUFS = 4
STEPS_PER_TRIP = 12
COL_BLOCK = 256
ONES_ROWS = 16

Q_SCALE_LOG2E = (HEAD_DIM ** -0.5) * math.log2(math.e)


def _sigmoid(x):
    return 1.0 / (1.0 + jnp.exp(-x))


def _gelu_tanh(x):
    c = math.sqrt(2.0 / math.pi)
    return 0.5 * x * (1.0 + jnp.tanh(c * (x + 0.044715 * (x * x * x))))


def _modulate(x, g, shift, scale):
    ms = jnp.mean(x * x, axis=-1, keepdims=True)
    return (x * lax.rsqrt(ms + EPS) * g) * (1.0 + scale) + shift


def _const_spec(shape):
    return pl.BlockSpec(shape, lambda *_: (0,) * len(shape), pipeline_mode=pl.Buffered(1))


def _adaln_kernel(c_ref, w_ref, b_ref, o_ref):
    c = c_ref[...]
    a = (c * _sigmoid(c)).astype(BF16)
    o_ref[...] = jnp.dot(a, w_ref[...].astype(BF16), preferred_element_type=F32) + b_ref[...]


def _adaln(c_all, w_mod, b_mod):
    rows, d = c_all.shape
    n = w_mod.shape[1]
    tn = 1024
    return pl.pallas_call(
        _adaln_kernel,
        grid=(n // tn,),
        in_specs=[
            pl.BlockSpec((rows, d), lambda j: (0, 0)),
            pl.BlockSpec((d, tn), lambda j: (0, j)),
            pl.BlockSpec((1, tn), lambda j: (0, j)),
        ],
        out_specs=pl.BlockSpec((rows, tn), lambda j: (0, j)),
        out_shape=jax.ShapeDtypeStruct((rows, n), F32),
        compiler_params=pltpu.CompilerParams(
            dimension_semantics=("arbitrary",), vmem_limit_bytes=V7X_VMEM_LIMIT_BYTES),
        name="adaln",
    )(c_all, w_mod, b_mod.reshape(1, n))


def _head_norm_t(blk, g_col):
    ms = jnp.mean(blk * blk, axis=0, keepdims=True)
    return blk * lax.rsqrt(ms + EPS) * g_col


def _rope_t(y, cos_t, sin_t):
    y1, y2 = y[:ROT_AXIS_DIM], y[ROT_AXIS_DIM:]
    return jnp.concatenate([y1 * cos_t - y2 * sin_t, y2 * cos_t + y1 * sin_t], axis=0)


def _in_proj_kernel(x_next_ref, ctx_ref, mod_ctx_ref, mod_ref, g1_ref, w_ref, kg_ref, qg_ref, gmg_ref,
                    cos_ref, sin_ref, k_ref, vt_ref, qt_ref, u_ref, vg_ref, ga_ref, gb_ref, h_ref):
    @pl.when(pl.program_id(1) == 0)
    def _():
        mod_ctx = mod_ctx_ref[0]
        h_ref[...] = _modulate(ctx_ref[0], g1_ref[...], mod_ctx[0:1], mod_ctx[1:2]).astype(BF16)

    p = jnp.dot(h_ref[...], w_ref[...], preferred_element_type=F32)

    cos_t = cos_ref[...]
    sin_t = sin_ref[...]
    pt = p[:, :U_OFF].T
    kg = kg_ref[...]
    kt = jnp.concatenate(
        [_rope_t(_head_norm_t(pt[HEAD_DIM * j:HEAD_DIM * (j + 1)], kg), cos_t, sin_t)
         for j in range(N_KV_HEADS)], axis=0)
    k_ref[0] = kt.T.astype(BF16)
    for j in range(N_KV_HEADS):
        vt_ref[0, j] = pt[K_W + HEAD_DIM * j:K_W + HEAD_DIM * (j + 1)].astype(BF16)
    qg = qg_ref[...]
    for j in range(N_Q_HEADS):
        blk = pt[Q_OFF + HEAD_DIM * j:Q_OFF + HEAD_DIM * (j + 1)]
        qt_ref[0, j] = (_rope_t(_head_norm_t(blk, qg), cos_t, sin_t) * Q_SCALE_LOG2E).astype(BF16)

    u_ref[0] = _gelu_tanh(p[:, U_OFF:VG_OFF]).astype(BF16)
    vgt = _gelu_tanh(p[:, VG_OFF:GA_OFF]).T
    gmg = gmg_ref[...]
    vgn_t = jnp.concatenate(
        [_head_norm_t(vgt[GM_GROUP_DIM * g:GM_GROUP_DIM * (g + 1)], gmg[GM_GROUP_DIM * g:GM_GROUP_DIM * (g + 1)])
         for g in range(GM_GROUPS)], axis=0)
    vg_ref[0] = vgn_t.T.astype(BF16)
    ga_ref[0] = p[:, GA_OFF:GB_OFF].astype(BF16)
    gb_ref[0] = p[:, GB_OFF:D_IN].astype(BF16)

    mod = mod_ref[0]
    h_ref[...] = _modulate(x_next_ref[0], g1_ref[...], mod[0:1], mod[1:2]).astype(BF16)


def _in_proj(x, ctx, mods3, norm1_g, w_in, k_norm_col, q_norm_col, gm_norm_col, cos_t, sin_t):
    b, n, d = x.shape
    tm = TOK_TILE
    assert ctx.shape == (b, tm, d)
    n_keys = n + tm
    lat = lambda t: jnp.maximum(t - 1, 0)
    tok = lambda w: pl.BlockSpec((1, tm, w), lambda i, t: (i, lat(t), 0))
    return pl.pallas_call(
        _in_proj_kernel,
        grid=(b, n // tm + 1),
        in_specs=[
            pl.BlockSpec((1, tm, d), lambda i, t: (i, jnp.minimum(t, n // tm - 1), 0)),
            pl.BlockSpec((1, tm, d), lambda i, t: (i, 0, 0)),
            pl.BlockSpec((1, N_MOD, d), lambda i, t: (b, 0, 0)),
            pl.BlockSpec((1, N_MOD, d), lambda i, t: (i, 0, 0)),
            _const_spec((1, d)),
            _const_spec((d, D_IN)),
            _const_spec((HEAD_DIM, 1)),
            _const_spec((HEAD_DIM, 1)),
            _const_spec((GM_WIDTH, 1)),
            pl.BlockSpec((ROT_AXIS_DIM, tm), lambda i, t: (0, t)),
            pl.BlockSpec((ROT_AXIS_DIM, tm), lambda i, t: (0, t)),
        ],
        out_specs=[
            pl.BlockSpec((1, tm, K_W), lambda i, t: (i, t, 0)),
            pl.BlockSpec((1, N_KV_HEADS, HEAD_DIM, tm), lambda i, t: (i, 0, 0, t)),
            pl.BlockSpec((1, N_Q_HEADS, HEAD_DIM, tm), lambda i, t: (i, 0, 0, lat(t))),
            tok(GM_WIDTH), tok(GM_WIDTH), tok(d), tok(d),
        ],
        out_shape=[
            jax.ShapeDtypeStruct((b, n_keys, K_W), BF16),
            jax.ShapeDtypeStruct((b, N_KV_HEADS, HEAD_DIM, n_keys), BF16),
            jax.ShapeDtypeStruct((b, N_Q_HEADS, HEAD_DIM, n), BF16),
            jax.ShapeDtypeStruct((b, n, GM_WIDTH), BF16),
            jax.ShapeDtypeStruct((b, n, GM_WIDTH), BF16),
            jax.ShapeDtypeStruct((b, n, d), BF16),
            jax.ShapeDtypeStruct((b, n, d), BF16),
        ],
        scratch_shapes=[pltpu.VMEM((tm, d), BF16)],
        compiler_params=pltpu.CompilerParams(
            dimension_semantics=("arbitrary", "arbitrary"), vmem_limit_bytes=V7X_VMEM_LIMIT_BYTES),
        name="in_proj",
    )(x, ctx, mods3, mods3, norm1_g, w_in, k_norm_col, q_norm_col, gm_norm_col, cos_t, sin_t)


def _attention_kernel(qt_ref, k_ref, vt_ref, o_ref, qz_ref, m_ref, acc_ref, *s_refs):
    def one_kv_head(kv_head, carry):
        _attention_kv_head(kv_head, qt_ref, k_ref, vt_ref, o_ref, qz_ref, m_ref, acc_ref, s_refs)
        return carry

    lax.fori_loop(0, N_KV_HEADS, one_kv_head, 0)


def _attention_kv_head(kv_head, qt_ref, k_ref, vt_ref, o_ref, qz_ref, m_ref, acc_ref, s_refs):
    tq = qt_ref.shape[3]
    n_tiles = k_ref.shape[1] // KV_TILE

    q_cat = jnp.concatenate([qt_ref[0, Q_PER_KV * kv_head + g] for g in range(Q_PER_KV)], axis=1)
    zeros = jnp.zeros_like(q_cat)
    qz_ref[...] = jnp.concatenate(
        [jnp.where(kv_head == 0, q_cat, zeros), jnp.where(kv_head == 1, q_cat, zeros)], axis=0)

    def tile_slice(i):
        return pl.ds(i * KV_TILE if isinstance(i, int) else pl.multiple_of(i * KV_TILE, KV_TILE), KV_TILE)

    col_blocks = [slice(h * COL_BLOCK, (h + 1) * COL_BLOCK) for h in range(m_ref.shape[1] // COL_BLOCK)]

    def step(i_prod, buf_prod, i_cons, buf_cons, s_max):
        if i_prod is not None:
            k_t = k_ref[0, tile_slice(i_prod), :]
        if i_cons is not None:
            vt = vt_ref[0, kv_head, :, tile_slice(i_cons)]
            v1t = jnp.concatenate([vt, jnp.ones((ONES_ROWS, KV_TILE), vt.dtype)], axis=0)
        new_max = []
        for h, cols in enumerate(col_blocks):
            if i_prod is not None:
                s = jnp.dot(k_t, qz_ref[:, cols], preferred_element_type=F32)
                s_refs[buf_prod][:, cols] = s
                new_max.append(jnp.max(s, axis=0, keepdims=True))
            if i_cons is not None:
                m_old = m_ref[:, cols]
                m_new = jnp.maximum(m_old, s_max[h])
                m_ref[:, cols] = m_new
                p = jnp.exp2(s_refs[buf_cons][:, cols] - m_new).astype(BF16)
                acc_ref[:, cols] = jnp.exp2(m_old - m_new) * acc_ref[:, cols] + jnp.dot(
                    v1t, p, preferred_element_type=F32)
        return tuple(new_max)

    m_ref[...] = jnp.full(m_ref.shape, -jnp.inf, F32)
    acc_ref[...] = jnp.zeros(acc_ref.shape, F32)

    def steps(first_tile, count, maxes):
        max_0, max_1 = maxes
        for j in range(count):
            max_2 = step(first_tile + j + 2, (j + 2) % N_SCORE_BUFS, first_tile + j, j % N_SCORE_BUFS, max_0)
            max_0, max_1 = max_1, max_2
        return max_0, max_1

    assert STEPS_PER_TRIP % N_SCORE_BUFS == 0
    maxes = (step(0, 0, None, None, None), step(1, 1, None, None, None))
    n_steps = n_tiles - 2
    n_trips = n_steps // STEPS_PER_TRIP
    maxes = lax.fori_loop(0, n_trips, lambda t, mx: steps(t * STEPS_PER_TRIP, STEPS_PER_TRIP, mx), maxes)
    done = n_trips * STEPS_PER_TRIP
    maxes = steps(done, n_steps - done, maxes)
    for j in range(2):
        step(None, None, n_steps + j, (n_steps + j) % N_SCORE_BUFS, maxes[j])

    acc = acc_ref[...]
    o_t = acc[:HEAD_DIM] / acc[HEAD_DIM:HEAD_DIM + 1]
    pairs = []
    for g in range(0, Q_PER_KV, 2):
        two = jnp.concatenate([o_t[:, g * tq:(g + 1) * tq], o_t[:, (g + 1) * tq:(g + 2) * tq]], axis=0)
        pairs.append(two.T)
    width = Q_PER_KV * HEAD_DIM
    o_ref[0, :, pl.ds(pl.multiple_of(kv_head * width, width), width)] = (
        jnp.concatenate(pairs, axis=1).astype(o_ref.dtype))


def _attention(qt, k, vt):
    b, _, _, n = qt.shape
    n_keys = k.shape[1]
    assert n_keys % KV_TILE == 0
    tq = Q_TILE
    cols = Q_PER_KV * tq
    return pl.pallas_call(
        _attention_kernel,
        grid=(b, n // tq),
        in_specs=[
            pl.BlockSpec((1, N_Q_HEADS, HEAD_DIM, tq), lambda i, t: (i, 0, 0, t)),
            pl.BlockSpec((1, n_keys, K_W), lambda i, t: (i, 0, 0)),
            pl.BlockSpec((1, N_KV_HEADS, HEAD_DIM, n_keys), lambda i, t: (i, 0, 0, 0)),
        ],
        out_specs=pl.BlockSpec((1, tq, ATTN_WIDTH), lambda i, t: (i, t, 0)),
        out_shape=jax.ShapeDtypeStruct((b, n, ATTN_WIDTH), BF16),
        scratch_shapes=[
            pltpu.VMEM((K_W, cols), BF16),
            pltpu.VMEM((1, cols), F32),
            pltpu.VMEM((HEAD_DIM + ONES_ROWS, cols), F32),
        ] + [pltpu.VMEM((KV_TILE, cols), F32)] * N_SCORE_BUFS,
        compiler_params=pltpu.CompilerParams(
            dimension_semantics=("arbitrary", "arbitrary"),
            vmem_limit_bytes=V7X_VMEM_LIMIT_BYTES),
        name="attention",
    )(qt, k, vt)


def _post_kernel(x_ref, mod_ref, attn_ref, u_ref, vg_ref, ga_ref, gb_ref, ws_ref, bs_ref,
                 wa_ref, wg_ref, wo_ref, g2_ref, w1_ref, w2_ref, o_ref):
    tm = x_ref.shape[1]
    mod = mod_ref[0]
    gate1, shift2, scale2, gate2 = mod[2:3], mod[3:4], mod[4:5], mod[5:6]

    lane = lax.broadcasted_iota(jnp.int32, (CHUNK, 2 * GM_GROUP_DIM), 1)
    first_half = lane < GM_GROUP_DIM
    vg = vg_ref[0]
    chunks = []
    for c in range(tm // CHUNK):
        cols = []
        for gp in range(GM_GROUPS // 2):
            rhs = vg[c * CHUNK:(c + 1) * CHUNK, gp * 128:(gp + 1) * 128]
            s0 = jnp.dot(ws_ref[2 * gp], rhs, preferred_element_type=F32)
            s1 = jnp.dot(ws_ref[2 * gp + 1], rhs, preferred_element_type=F32)
            cols.append(jnp.where(first_half, s0, s1))
        chunks.append(jnp.concatenate(cols, axis=1) + bs_ref[...])
    s = jnp.concatenate(chunks, axis=0)
    gm = (u_ref[0].astype(F32) * s).astype(BF16)

    y = (_sigmoid(ga_ref[0].astype(F32)) * jnp.dot(attn_ref[0], wa_ref[...], preferred_element_type=F32)
         + _sigmoid(gb_ref[0].astype(F32)) * jnp.dot(gm, wg_ref[...], preferred_element_type=F32))
    x1 = x_ref[0] + gate1 * jnp.dot(y.astype(BF16), wo_ref[...], preferred_element_type=F32)

    h2 = _modulate(x1, g2_ref[...], shift2, scale2).astype(BF16)
    f = jnp.maximum(jnp.dot(h2, w1_ref[...], preferred_element_type=F32), 0.0)
    f = (f * f).astype(BF16)
    o_ref[0] = x1 + gate2 * jnp.dot(f, w2_ref[...], preferred_element_type=F32)


def _post(x, mods3, attn, u, vg, ga, gb, gm_ws, bs_full, w_br_attn, w_br_gm, w_out, norm2_g, w_ff1, w_ff2):
    b, n, d = x.shape
    tm = POST_TILE
    tok = lambda w: pl.BlockSpec((1, tm, w), lambda i, t: (i, t, 0))
    return pl.pallas_call(
        _post_kernel,
        grid=(b, n // tm),
        in_specs=[
            tok(d),
            pl.BlockSpec((1, N_MOD, d), lambda i, t: (i, 0, 0)),
            tok(ATTN_WIDTH), tok(GM_WIDTH), tok(GM_WIDTH), tok(d), tok(d),
            _const_spec((GM_GROUPS, CHUNK, CHUNK)),
            _const_spec((CHUNK, GM_WIDTH)),
            _const_spec((ATTN_WIDTH, d)),
            _const_spec((GM_WIDTH, d)),
            _const_spec((d, d)),
            _const_spec((1, d)),
            _const_spec((d, D_FF)),
            _const_spec((D_FF, d)),
        ],
        out_specs=tok(d),
        out_shape=jax.ShapeDtypeStruct((b, n, d), F32),
        compiler_params=pltpu.CompilerParams(
            dimension_semantics=("arbitrary", "arbitrary"), vmem_limit_bytes=V7X_VMEM_LIMIT_BYTES),
        name="post",
    )(x, mods3, attn, u, vg, ga, gb, gm_ws, bs_full, w_br_attn, w_br_gm, w_out, norm2_g, w_ff1, w_ff2)


def _rope_tables_t(n_tokens, n_ctx):
    rows = n_tokens // GRID_W
    row = jnp.repeat(jnp.arange(rows, dtype=F32), GRID_W)
    col = jnp.tile(jnp.arange(GRID_W, dtype=F32), rows)
    inv = ROPE_THETA ** (-jnp.arange(0, ROT_AXIS_DIM, 2, dtype=F32) / ROT_AXIS_DIM)
    ang_t = jnp.concatenate([inv[:, None] * row[None, :], inv[:, None] * col[None, :]], axis=0)
    cos_t = jnp.concatenate([jnp.ones((ROT_AXIS_DIM, n_ctx), F32), jnp.cos(ang_t)], axis=1)
    sin_t = jnp.concatenate([jnp.zeros((ROT_AXIS_DIM, n_ctx), F32), jnp.sin(ang_t)], axis=1)
    return cos_t, sin_t


def kernel(x, c, ctx, c_ctx, w_mod, b_mod, norm1_g, norm2_g, w_in, q_norm_g, k_norm_g, gm_norm_g,
           gm_ws, gm_bs, w_br_attn, w_br_gm, w_out, w_ff1, w_ff2):
    b, n, d = x.shape
    assert w_mod.shape[0] == 1, "single-layer block"
    assert (b, d) == c.shape and d == D_MODEL and n % max(TOK_TILE, Q_TILE) == 0

    pad_rows = (-(b + 1)) % 8
    c_all = jnp.concatenate([c, c_ctx[None, :], jnp.zeros((pad_rows, d), c.dtype)], axis=0)
    mods3 = _adaln(c_all, w_mod[0], b_mod[0]).reshape(b + 1 + pad_rows, N_MOD, d)

    w_in_b = w_in[0].astype(BF16)
    k_norm_col = k_norm_g[0].reshape(HEAD_DIM, 1)
    q_norm_col = q_norm_g[0].reshape(HEAD_DIM, 1)
    gm_norm_col = gm_norm_g[0].reshape(GM_WIDTH, 1)
    cos_t, sin_t = _rope_tables_t(n, ctx.shape[1])

    k, vt, qt, u, vg, ga, gb = _in_proj(
        x, ctx, mods3, norm1_g, w_in_b, k_norm_col, q_norm_col, gm_norm_col, cos_t, sin_t)
    attn = _attention(qt, k, vt)

    bs_full = jnp.repeat(gm_bs[0].T, GM_GROUP_DIM, axis=1)
    return _post(x, mods3, attn, u, vg, ga, gb, gm_ws[0].astype(BF16), bs_full,
                 w_br_attn[0].astype(BF16), w_br_gm[0].astype(BF16), w_out[0].astype(BF16),
                 norm2_g, w_ff1[0].astype(BF16), w_ff2[0].astype(BF16))
```

```python
import functools
import math

import jax
import jax.numpy as jnp
from jax import lax
from jax.experimental import pallas as pl
from jax.experimental.pallas import tpu as pltpu

F32 = jnp.float32
BF16 = jnp.bfloat16

D_MODEL = 1024
GRID_W = 64
HEAD_DIM = 64
N_Q_HEADS = 8
N_KV_HEADS = 2
Q_PER_KV = N_Q_HEADS // N_KV_HEADS
ATTN_WIDTH = N_Q_HEADS * HEAD_DIM
GM_GROUPS = 8
GM_GROUP_DIM = 64
GM_WIDTH = GM_GROUPS * GM_GROUP_DIM
CHUNK = 128
D_FF = 4 * D_MODEL
ROPE_THETA = 10000.0
ROT_AXIS_DIM = HEAD_DIM // 2
EPS = 1e-6
N_MOD = 6

K_W = N_KV_HEADS * HEAD_DIM
KV_COLS = 2 * K_W
Q_OFF = KV_COLS
U_OFF = Q_OFF + ATTN_WIDTH
VG_OFF = U_OFF + GM_WIDTH
GA_OFF = VG_OFF + GM_WIDTH
GB_OFF = GA_OFF + D_MODEL
D_IN = GB_OFF + D_MODEL

V7X_VMEM_LIMIT_BYTES = 56 * 1024 * 1024

TOK_TILE = 256
POST_TILE = 512
Q_TILE = 1024
KV_TILE = 256
PIPE_DEPTH = 1
N_SCORE_BUFS = 2
STEPS_PER_TRIP = 12
COL_BLOCK = 256
ONES_ROWS = 16

Q_SCALE_LOG2E = (HEAD_DIM ** -0.5) * math.log2(math.e)


def _sigmoid(x):
    return 1.0 / (1.0 + jnp.exp(-x))


def _gelu_tanh(x):
    c = math.sqrt(2.0 / math.pi)
    return 0.5 * x * (1.0 + jnp.tanh(c * (x + 0.044715 * (x * x * x))))


def _modulate(x, g, shift, scale):
    ms = jnp.mean(x * x, axis=-1, keepdims=True)
    return (x * lax.rsqrt(ms + EPS) * g) * (1.0 + scale) + shift


def _const_spec(shape):
    return pl.BlockSpec(shape, lambda *_: (0,) * len(shape), pipeline_mode=pl.Buffered(1))


def _adaln_kernel(c_ref, w_ref, b_ref, o_ref):
    c = c_ref[...]
    a = (c * _sigmoid(c)).astype(BF16)
    o_ref[...] = jnp.dot(a, w_ref[...].astype(BF16), preferred_element_type=F32) + b_ref[...]


def _adaln(c_all, w_mod, b_mod):
    rows, d = c_all.shape
    n = w_mod.shape[1]
    tn = 1024
    return pl.pallas_call(
        _adaln_kernel,
        grid=(n // tn,),
        in_specs=[
            pl.BlockSpec((rows, d), lambda j: (0, 0)),
            pl.BlockSpec((d, tn), lambda j: (0, j)),
            pl.BlockSpec((1, tn), lambda j: (0, j)),
        ],
        out_specs=pl.BlockSpec((rows, tn), lambda j: (0, j)),
        out_shape=jax.ShapeDtypeStruct((rows, n), F32),
        compiler_params=pltpu.CompilerParams(
            dimension_semantics=("arbitrary",), vmem_limit_bytes=V7X_VMEM_LIMIT_BYTES),
        name="adaln",
    )(c_all, w_mod, b_mod.reshape(1, n))


def _head_norm_t(blk, g_col):
    ms = jnp.mean(blk * blk, axis=0, keepdims=True)
    return blk * lax.rsqrt(ms + EPS) * g_col


def _rope_t(y, cos_t, sin_t):
    y1, y2 = y[:ROT_AXIS_DIM], y[ROT_AXIS_DIM:]
    return jnp.concatenate([y1 * cos_t - y2 * sin_t, y2 * cos_t + y1 * sin_t], axis=0)


def _in_proj_kernel(x_next_ref, ctx_ref, mod_ctx_ref, mod_ref, g1_ref, w_ref, kg_ref, qg_ref, gmg_ref,
                    cos_ref, sin_ref, k_ref, vt_ref, qt_ref, u_ref, vg_ref, ga_ref, gb_ref, h_ref):
    @pl.when(pl.program_id(1) == 0)
    def _():
        mod_ctx = mod_ctx_ref[0]
        h_ref[...] = _modulate(ctx_ref[0], g1_ref[...], mod_ctx[0:1], mod_ctx[1:2]).astype(BF16)

    p = jnp.dot(h_ref[...], w_ref[...], preferred_element_type=F32)

    cos_t = cos_ref[...]
    sin_t = sin_ref[...]
    pt = p[:, :U_OFF].T
    kg = kg_ref[...]
    kt = jnp.concatenate(
        [_rope_t(_head_norm_t(pt[HEAD_DIM * j:HEAD_DIM * (j + 1)], kg), cos_t, sin_t)
         for j in range(N_KV_HEADS)], axis=0)
    k_ref[0] = kt.T.astype(BF16)
    for j in range(N_KV_HEADS):
        vt_ref[0, j] = pt[K_W + HEAD_DIM * j:K_W + HEAD_DIM * (j + 1)].astype(BF16)
    qg = qg_ref[...]
    for j in range(N_Q_HEADS):
        blk = pt[Q_OFF + HEAD_DIM * j:Q_OFF + HEAD_DIM * (j + 1)]
        qt_ref[0, j] = (_rope_t(_head_norm_t(blk, qg), cos_t, sin_t) * Q_SCALE_LOG2E).astype(BF16)

    u_ref[0] = _gelu_tanh(p[:, U_OFF:VG_OFF]).astype(BF16)
    vgt = _gelu_tanh(p[:, VG_OFF:GA_OFF]).T
    gmg = gmg_ref[...]
    vgn_t = jnp.concatenate(
        [_head_norm_t(vgt[GM_GROUP_DIM * g:GM_GROUP_DIM * (g + 1)], gmg[GM_GROUP_DIM * g:GM_GROUP_DIM * (g + 1)])
         for g in range(GM_GROUPS)], axis=0)
    vg_ref[0] = vgn_t.T.astype(BF16)
    ga_ref[0] = p[:, GA_OFF:GB_OFF].astype(BF16)
    gb_ref[0] = p[:, GB_OFF:D_IN].astype(BF16)

    mod = mod_ref[0]
    h_ref[...] = _modulate(x_next_ref[0], g1_ref[...], mod[0:1], mod[1:2]).astype(BF16)


def _in_proj(x, ctx, mods3, norm1_g, w_in, k_norm_col, q_norm_col, gm_norm_col, cos_t, sin_t):
    b, n, d = x.shape
    tm = TOK_TILE
    assert ctx.shape == (b, tm, d)
    n_keys = n + tm
    lat = lambda t: jnp.maximum(t - 1, 0)
    tok = lambda w: pl.BlockSpec((1, tm, w), lambda i, t: (i, lat(t), 0))
    return pl.pallas_call(
        _in_proj_kernel,
        grid=(b, n // tm + 1),
        in_specs=[
            pl.BlockSpec((1, tm, d), lambda i, t: (i, jnp.minimum(t, n // tm - 1), 0)),
            pl.BlockSpec((1, tm, d), lambda i, t: (i, 0, 0)),
            pl.BlockSpec((1, N_MOD, d), lambda i, t: (b, 0, 0)),
            pl.BlockSpec((1, N_MOD, d), lambda i, t: (i, 0, 0)),
            _const_spec((1, d)),
            _const_spec((d, D_IN)),
            _const_spec((HEAD_DIM, 1)),
            _const_spec((HEAD_DIM, 1)),
            _const_spec((GM_WIDTH, 1)),
            pl.BlockSpec((ROT_AXIS_DIM, tm), lambda i, t: (0, t)),
            pl.BlockSpec((ROT_AXIS_DIM, tm), lambda i, t: (0, t)),
        ],
        out_specs=[
            pl.BlockSpec((1, tm, K_W), lambda i, t: (i, t, 0)),
            pl.BlockSpec((1, N_KV_HEADS, HEAD_DIM, tm), lambda i, t: (i, 0, 0, t)),
            pl.BlockSpec((1, N_Q_HEADS, HEAD_DIM, tm), lambda i, t: (i, 0, 0, lat(t))),
            tok(GM_WIDTH), tok(GM_WIDTH), tok(d), tok(d),
        ],
        out_shape=[
            jax.ShapeDtypeStruct((b, n_keys, K_W), BF16),
            jax.ShapeDtypeStruct((b, N_KV_HEADS, HEAD_DIM, n_keys), BF16),
            jax.ShapeDtypeStruct((b, N_Q_HEADS, HEAD_DIM, n), BF16),
            jax.ShapeDtypeStruct((b, n, GM_WIDTH), BF16),
            jax.ShapeDtypeStruct((b, n, GM_WIDTH), BF16),
            jax.ShapeDtypeStruct((b, n, d), BF16),
            jax.ShapeDtypeStruct((b, n, d), BF16),
        ],
        scratch_shapes=[pltpu.VMEM((tm, d), BF16)],
        compiler_params=pltpu.CompilerParams(
            dimension_semantics=("arbitrary", "arbitrary"), vmem_limit_bytes=V7X_VMEM_LIMIT_BYTES),
        name="in_proj",
    )(x, ctx, mods3, mods3, norm1_g, w_in, k_norm_col, q_norm_col, gm_norm_col, cos_t, sin_t)


def _attention_kernel(qt_ref, k_ref, vt_ref, o_ref, qz_ref, m_ref, acc_ref, *s_refs):
    def one_kv_head(kv_head, carry):
        _attention_kv_head(kv_head, qt_ref, k_ref, vt_ref, o_ref, qz_ref, m_ref, acc_ref, s_refs)
        return carry

    lax.fori_loop(0, N_KV_HEADS, one_kv_head, 0)


def _attention_kv_head(kv_head, qt_ref, k_ref, vt_ref, o_ref, qz_ref, m_ref, acc_ref, s_refs):
    tq = qt_ref.shape[3]
    n_tiles = k_ref.shape[1] // KV_TILE

    q_cat = jnp.concatenate([qt_ref[0, Q_PER_KV * kv_head + g] for g in range(Q_PER_KV)], axis=1)
    zeros = jnp.zeros_like(q_cat)
    qz_ref[...] = jnp.concatenate(
        [jnp.where(kv_head == 0, q_cat, zeros), jnp.where(kv_head == 1, q_cat, zeros)], axis=0)

    def tile_slice(i):
        return pl.ds(i * KV_TILE if isinstance(i, int) else pl.multiple_of(i * KV_TILE, KV_TILE), KV_TILE)

    col_blocks = [slice(h * COL_BLOCK, (h + 1) * COL_BLOCK) for h in range(m_ref.shape[1] // COL_BLOCK)]

    def step(i_prod, buf_prod, i_cons, buf_cons, s_max):
        if i_prod is not None:
            k_t = k_ref[0, tile_slice(i_prod), :]
        if i_cons is not None:
            vt = vt_ref[0, kv_head, :, tile_slice(i_cons)]
            v1t = jnp.concatenate([vt, jnp.ones((ONES_ROWS, KV_TILE), vt.dtype)], axis=0)
        new_max = []
        for h, cols in enumerate(col_blocks):
            if i_prod is not None:
                s = jnp.dot(k_t, qz_ref[:, cols], preferred_element_type=F32)
                s_refs[buf_prod][:, cols] = s
                new_max.append(jnp.max(s, axis=0, keepdims=True))
            if i_cons is not None:
                m_old = m_ref[:, cols]
                m_new = jnp.maximum(m_old, s_max[h])
                m_ref[:, cols] = m_new
                p = jnp.exp2(s_refs[buf_cons][:, cols] - m_new).astype(BF16)
                acc_ref[:, cols] = jnp.exp2(m_old - m_new) * acc_ref[:, cols] + jnp.dot(
                    v1t, p, preferred_element_type=F32)
        return tuple(new_max)

    m_ref[...] = jnp.full(m_ref.shape, -jnp.inf, F32)
    acc_ref[...] = jnp.zeros(acc_ref.shape, F32)

    depth = PIPE_DEPTH

    def steps(first_tile, count, maxes):
        maxes = list(maxes)
        for j in range(count):
            new = step(first_tile + j + depth, (j + depth) % N_SCORE_BUFS, first_tile + j, j % N_SCORE_BUFS, maxes[0])
            maxes = maxes[1:] + [new]
        return tuple(maxes)

    assert STEPS_PER_TRIP % N_SCORE_BUFS == 0 and N_SCORE_BUFS > depth
    maxes = tuple(step(i, i, None, None, None) for i in range(depth))
    n_steps = n_tiles - depth
    n_trips = n_steps // STEPS_PER_TRIP
    maxes = lax.fori_loop(0, n_trips, lambda t, mx: steps(t * STEPS_PER_TRIP, STEPS_PER_TRIP, mx), maxes)
    done = n_trips * STEPS_PER_TRIP
    maxes = steps(done, n_steps - done, maxes)
    for j in range(depth):
        step(None, None, n_steps + j, (n_steps + j) % N_SCORE_BUFS, maxes[j])

    acc = acc_ref[...]
    o_t = acc[:HEAD_DIM] / acc[HEAD_DIM:HEAD_DIM + 1]
    pairs = []
    for g in range(0, Q_PER_KV, 2):
        two = jnp.concatenate([o_t[:, g * tq:(g + 1) * tq], o_t[:, (g + 1) * tq:(g + 2) * tq]], axis=0)
        pairs.append(two.T)
    width = Q_PER_KV * HEAD_DIM
    o_ref[0, :, pl.ds(pl.multiple_of(kv_head * width, width), width)] = (
        jnp.concatenate(pairs, axis=1).astype(o_ref.dtype))


def _attention(qt, k, vt):
    b, _, _, n = qt.shape
    n_keys = k.shape[1]
    assert n_keys % KV_TILE == 0
    tq = Q_TILE
    cols = Q_PER_KV * tq
    return pl.pallas_call(
        _attention_kernel,
        grid=(b, n // tq),
        in_specs=[
            pl.BlockSpec((1, N_Q_HEADS, HEAD_DIM, tq), lambda i, t: (i, 0, 0, t)),
            pl.BlockSpec((1, n_keys, K_W), lambda i, t: (i, 0, 0)),
            pl.BlockSpec((1, N_KV_HEADS, HEAD_DIM, n_keys), lambda i, t: (i, 0, 0, 0)),
        ],
        out_specs=pl.BlockSpec((1, tq, ATTN_WIDTH), lambda i, t: (i, t, 0)),
        out_shape=jax.ShapeDtypeStruct((b, n, ATTN_WIDTH), BF16),
        scratch_shapes=[
            pltpu.VMEM((K_W, cols), BF16),
            pltpu.VMEM((1, cols), F32),
            pltpu.VMEM((HEAD_DIM + ONES_ROWS, cols), F32),
        ] + [pltpu.VMEM((KV_TILE, cols), F32)] * N_SCORE_BUFS,
        compiler_params=pltpu.CompilerParams(
            dimension_semantics=("arbitrary", "arbitrary"),
            vmem_limit_bytes=V7X_VMEM_LIMIT_BYTES),
        name="attention",
    )(qt, k, vt)


def _post_kernel(x_ref, mod_ref, attn_ref, u_ref, vg_ref, ga_ref, gb_ref, ws_ref, bs_ref,
                 wa_ref, wg_ref, wo_ref, g2_ref, w1_ref, w2_ref, o_ref):
    tm = x_ref.shape[1]
    mod = mod_ref[0]
    gate1, shift2, scale2, gate2 = mod[2:3], mod[3:4], mod[4:5], mod[5:6]

    lane = lax.broadcasted_iota(jnp.int32, (CHUNK, 2 * GM_GROUP_DIM), 1)
    first_half = lane < GM_GROUP_DIM
    vg = vg_ref[0]
    chunks = []
    for c in range(tm // CHUNK):
        cols = []
        for gp in range(GM_GROUPS // 2):
            rhs = vg[c * CHUNK:(c + 1) * CHUNK, gp * 128:(gp + 1) * 128]
            s0 = jnp.dot(ws_ref[2 * gp], rhs, preferred_element_type=F32)
            s1 = jnp.dot(ws_ref[2 * gp + 1], rhs, preferred_element_type=F32)
            cols.append(jnp.where(first_half, s0, s1))
        chunks.append(jnp.concatenate(cols, axis=1) + bs_ref[...])
    s = jnp.concatenate(chunks, axis=0)
    gm = (u_ref[0].astype(F32) * s).astype(BF16)

    y = (_sigmoid(ga_ref[0].astype(F32)) * jnp.dot(attn_ref[0], wa_ref[...], preferred_element_type=F32)
         + _sigmoid(gb_ref[0].astype(F32)) * jnp.dot(gm, wg_ref[...], preferred_element_type=F32))
    x1 = x_ref[0] + gate1 * jnp.dot(y.astype(BF16), wo_ref[...], preferred_element_type=F32)

    h2 = _modulate(x1, g2_ref[...], shift2, scale2).astype(BF16)
    f = jnp.maximum(jnp.dot(h2, w1_ref[...], preferred_element_type=F32), 0.0)
    f = (f * f).astype(BF16)
    o_ref[0] = x1 + gate2 * jnp.dot(f, w2_ref[...], preferred_element_type=F32)


def _post(x, mods3, attn, u, vg, ga, gb, gm_ws, bs_full, w_br_attn, w_br_gm, w_out, norm2_g, w_ff1, w_ff2):
    b, n, d = x.shape
    tm = POST_TILE
    tok = lambda w: pl.BlockSpec((1, tm, w), lambda i, t: (i, t, 0))
    return pl.pallas_call(
        _post_kernel,
        grid=(b, n // tm),
        in_specs=[
            tok(d),
            pl.BlockSpec((1, N_MOD, d), lambda i, t: (i, 0, 0)),
            tok(ATTN_WIDTH), tok(GM_WIDTH), tok(GM_WIDTH), tok(d), tok(d),
            _const_spec((GM_GROUPS, CHUNK, CHUNK)),
            _const_spec((CHUNK, GM_WIDTH)),
            _const_spec((ATTN_WIDTH, d)),
            _const_spec((GM_WIDTH, d)),
            _const_spec((d, d)),
            _const_spec((1, d)),
            _const_spec((d, D_FF)),
            _const_spec((D_FF, d)),
        ],
        out_specs=tok(d),
        out_shape=jax.ShapeDtypeStruct((b, n, d), F32),
        compiler_params=pltpu.CompilerParams(
            dimension_semantics=("arbitrary", "arbitrary"), vmem_limit_bytes=V7X_VMEM_LIMIT_BYTES),
        name="post",
    )(x, mods3, attn, u, vg, ga, gb, gm_ws, bs_full, w_br_attn, w_br_gm, w_out, norm2_g, w_ff1, w_ff2)


def _rope_tables_t(n_tokens, n_ctx):
    rows = n_tokens // GRID_W
    row = jnp.repeat(jnp.arange(rows, dtype=F32), GRID_W)
    col = jnp.tile(jnp.arange(GRID_W, dtype=F32), rows)
    inv = ROPE_THETA ** (-jnp.arange(0, ROT_AXIS_DIM, 2, dtype=F32) / ROT_AXIS_DIM)
    ang_t = jnp.concatenate([inv[:, None] * row[None, :], inv[:, None] * col[None, :]], axis=0)
    cos_t = jnp.concatenate([jnp.ones((ROT_AXIS_DIM, n_ctx), F32), jnp.cos(ang_t)], axis=1)
    sin_t = jnp.concatenate([jnp.zeros((ROT_AXIS_DIM, n_ctx), F32), jnp.sin(ang_t)], axis=1)
    return cos_t, sin_t


def kernel(x, c, ctx, c_ctx, w_mod, b_mod, norm1_g, norm2_g, w_in, q_norm_g, k_norm_g, gm_norm_g,
           gm_ws, gm_bs, w_br_attn, w_br_gm, w_out, w_ff1, w_ff2):
    b, n, d = x.shape
    assert w_mod.shape[0] == 1, "single-layer block"
    assert (b, d) == c.shape and d == D_MODEL and n % max(TOK_TILE, Q_TILE) == 0

    pad_rows = (-(b + 1)) % 8
    c_all = jnp.concatenate([c, c_ctx[None, :], jnp.zeros((pad_rows, d), c.dtype)], axis=0)
    mods3 = _adaln(c_all, w_mod[0], b_mod[0]).reshape(b + 1 + pad_rows, N_MOD, d)

    w_in_b = w_in[0].astype(BF16)
    k_norm_col = k_norm_g[0].reshape(HEAD_DIM, 1)
    q_norm_col = q_norm_g[0].reshape(HEAD_DIM, 1)
    gm_norm_col = gm_norm_g[0].reshape(GM_WIDTH, 1)
    cos_t, sin_t = _rope_tables_t(n, ctx.shape[1])

    k, vt, qt, u, vg, ga, gb = _in_proj(
        x, ctx, mods3, norm1_g, w_in_b, k_norm_col, q_norm_col, gm_norm_col, cos_t, sin_t)
    attn = _attention(qt, k, vt)

    bs_full = jnp.repeat(gm_bs[0].T, GM_GROUP_DIM, axis=1)
    return _post(x, mods3, attn, u, vg, ga, gb, gm_ws[0].astype(BF16), bs_full,
                 w_br_attn[0].astype(BF16), w_br_gm[0].astype(BF16), w_out[0].astype(BF16),
                 norm2_g, w_ff1[0].astype(BF16), w_ff2[0].astype(BF16))
```

```python
import functools
import math

import jax
import jax.numpy as jnp
from jax import lax
from jax.experimental import pallas as pl
from jax.experimental.pallas import tpu as pltpu

F32 = jnp.float32
BF16 = jnp.bfloat16

D_MODEL = 1024
GRID_W = 64
HEAD_DIM = 64
N_Q_HEADS = 8
N_KV_HEADS = 2
Q_PER_KV = N_Q_HEADS // N_KV_HEADS
ATTN_WIDTH = N_Q_HEADS * HEAD_DIM
GM_GROUPS = 8
GM_GROUP_DIM = 64
GM_WIDTH = GM_GROUPS * GM_GROUP_DIM
CHUNK = 128
D_FF = 4 * D_MODEL
ROPE_THETA = 10000.0
ROT_AXIS_DIM = HEAD_DIM // 2
EPS = 1e-6
N_MOD = 6

K_W = N_KV_HEADS * HEAD_DIM
KV_COLS = 2 * K_W
Q_OFF = KV_COLS
U_OFF = Q_OFF + ATTN_WIDTH
VG_OFF = U_OFF + GM_WIDTH
GA_OFF = VG_OFF + GM_WIDTH
GB_OFF = GA_OFF + D_MODEL
D_IN = GB_OFF + D_MODEL

V7X_VMEM_LIMIT_BYTES = 56 * 1024 * 1024

TOK_TILE = 256
POST_TILE = 512
Q_TILE = 1024
KV_TILE = 256
PIPE_DEPTH = 1
N_SCORE_BUFS = 2
STEPS_PER_TRIP = 14
COL_BLOCK = 256
ONES_ROWS = 16

Q_SCALE_LOG2E = (HEAD_DIM ** -0.5) * math.log2(math.e)


def _sigmoid(x):
    return 1.0 / (1.0 + jnp.exp(-x))


def _gelu_tanh(x):
    c = math.sqrt(2.0 / math.pi)
    return 0.5 * x * (1.0 + jnp.tanh(c * (x + 0.044715 * (x * x * x))))


def _modulate(x, g, shift, scale):
    ms = jnp.mean(x * x, axis=-1, keepdims=True)
    return (x * lax.rsqrt(ms + EPS) * g) * (1.0 + scale) + shift


def _const_spec(shape):
    return pl.BlockSpec(shape, lambda *_: (0,) * len(shape), pipeline_mode=pl.Buffered(1))


def _adaln_kernel(c_ref, w_ref, b_ref, o_ref):
    c = c_ref[...]
    a = (c * _sigmoid(c)).astype(BF16)
    o_ref[...] = jnp.dot(a, w_ref[...].astype(BF16), preferred_element_type=F32) + b_ref[...]


def _adaln(c_all, w_mod, b_mod):
    rows, d = c_all.shape
    n = w_mod.shape[1]
    tn = 1024
    return pl.pallas_call(
        _adaln_kernel,
        grid=(n // tn,),
        in_specs=[
            pl.BlockSpec((rows, d), lambda j: (0, 0)),
            pl.BlockSpec((d, tn), lambda j: (0, j)),
            pl.BlockSpec((1, tn), lambda j: (0, j)),
        ],
        out_specs=pl.BlockSpec((rows, tn), lambda j: (0, j)),
        out_shape=jax.ShapeDtypeStruct((rows, n), F32),
        compiler_params=pltpu.CompilerParams(
            dimension_semantics=("arbitrary",), vmem_limit_bytes=V7X_VMEM_LIMIT_BYTES),
        name="adaln",
    )(c_all, w_mod, b_mod.reshape(1, n))


def _head_norm_t(blk, g_col):
    ms = jnp.mean(blk * blk, axis=0, keepdims=True)
    return blk * lax.rsqrt(ms + EPS) * g_col


def _rope_t(y, cos_t, sin_t):
    y1, y2 = y[:ROT_AXIS_DIM], y[ROT_AXIS_DIM:]
    return jnp.concatenate([y1 * cos_t - y2 * sin_t, y2 * cos_t + y1 * sin_t], axis=0)


def _in_proj_kernel(x_next_ref, ctx_ref, mod_ctx_ref, mod_ref, g1_ref, w_ref, kg_ref, qg_ref, gmg_ref,
                    cos_ref, sin_ref, k_ref, vt_ref, qt_ref, u_ref, vg_ref, ga_ref, gb_ref, h_ref):
    @pl.when(pl.program_id(1) == 0)
    def _():
        mod_ctx = mod_ctx_ref[0]
        h_ref[...] = _modulate(ctx_ref[0], g1_ref[...], mod_ctx[0:1], mod_ctx[1:2]).astype(BF16)

    p = jnp.dot(h_ref[...], w_ref[...], preferred_element_type=F32)

    cos_t = cos_ref[...]
    sin_t = sin_ref[...]
    pt = p[:, :U_OFF].T
    kg = kg_ref[...]
    kt = jnp.concatenate(
        [_rope_t(_head_norm_t(pt[HEAD_DIM * j:HEAD_DIM * (j + 1)], kg), cos_t, sin_t)
         for j in range(N_KV_HEADS)], axis=0)
    k_ref[0] = kt.T.astype(BF16)
    for j in range(N_KV_HEADS):
        vt_ref[0, j] = pt[K_W + HEAD_DIM * j:K_W + HEAD_DIM * (j + 1)].astype(BF16)
    qg = qg_ref[...]
    for j in range(N_Q_HEADS):
        blk = pt[Q_OFF + HEAD_DIM * j:Q_OFF + HEAD_DIM * (j + 1)]
        qt_ref[0, j] = (_rope_t(_head_norm_t(blk, qg), cos_t, sin_t) * Q_SCALE_LOG2E).astype(BF16)

    u_ref[0] = _gelu_tanh(p[:, U_OFF:VG_OFF]).astype(BF16)
    vgt = _gelu_tanh(p[:, VG_OFF:GA_OFF]).T
    gmg = gmg_ref[...]
    vgn_t = jnp.concatenate(
        [_head_norm_t(vgt[GM_GROUP_DIM * g:GM_GROUP_DIM * (g + 1)], gmg[GM_GROUP_DIM * g:GM_GROUP_DIM * (g + 1)])
         for g in range(GM_GROUPS)], axis=0)
    vg_ref[0] = vgn_t.T.astype(BF16)
    ga_ref[0] = p[:, GA_OFF:GB_OFF].astype(BF16)
    gb_ref[0] = p[:, GB_OFF:D_IN].astype(BF16)

    mod = mod_ref[0]
    h_ref[...] = _modulate(x_next_ref[0], g1_ref[...], mod[0:1], mod[1:2]).astype(BF16)


def _in_proj(x, ctx, mods3, norm1_g, w_in, k_norm_col, q_norm_col, gm_norm_col, cos_t, sin_t):
    b, n, d = x.shape
    tm = TOK_TILE
    assert ctx.shape == (b, tm, d)
    n_keys = n + tm
    lat = lambda t: jnp.maximum(t - 1, 0)
    tok = lambda w: pl.BlockSpec((1, tm, w), lambda i, t: (i, lat(t), 0))
    return pl.pallas_call(
        _in_proj_kernel,
        grid=(b, n // tm + 1),
        in_specs=[
            pl.BlockSpec((1, tm, d), lambda i, t: (i, jnp.minimum(t, n // tm - 1), 0)),
            pl.BlockSpec((1, tm, d), lambda i, t: (i, 0, 0)),
            pl.BlockSpec((1, N_MOD, d), lambda i, t: (b, 0, 0)),
            pl.BlockSpec((1, N_MOD, d), lambda i, t: (i, 0, 0)),
            _const_spec((1, d)),
            _const_spec((d, D_IN)),
            _const_spec((HEAD_DIM, 1)),
            _const_spec((HEAD_DIM, 1)),
            _const_spec((GM_WIDTH, 1)),
            pl.BlockSpec((ROT_AXIS_DIM, tm), lambda i, t: (0, t)),
            pl.BlockSpec((ROT_AXIS_DIM, tm), lambda i, t: (0, t)),
        ],
        out_specs=[
            pl.BlockSpec((1, tm, K_W), lambda i, t: (i, t, 0)),
            pl.BlockSpec((1, N_KV_HEADS, HEAD_DIM, tm), lambda i, t: (i, 0, 0, t)),
            pl.BlockSpec((1, N_Q_HEADS, HEAD_DIM, tm), lambda i, t: (i, 0, 0, lat(t))),
            tok(GM_WIDTH), tok(GM_WIDTH), tok(d), tok(d),
        ],
        out_shape=[
            jax.ShapeDtypeStruct((b, n_keys, K_W), BF16),
            jax.ShapeDtypeStruct((b, N_KV_HEADS, HEAD_DIM, n_keys), BF16),
            jax.ShapeDtypeStruct((b, N_Q_HEADS, HEAD_DIM, n), BF16),
            jax.ShapeDtypeStruct((b, n, GM_WIDTH), BF16),
            jax.ShapeDtypeStruct((b, n, GM_WIDTH), BF16),
            jax.ShapeDtypeStruct((b, n, d), BF16),
            jax.ShapeDtypeStruct((b, n, d), BF16),
        ],
        scratch_shapes=[pltpu.VMEM((tm, d), BF16)],
        compiler_params=pltpu.CompilerParams(
            dimension_semantics=("arbitrary", "arbitrary"), vmem_limit_bytes=V7X_VMEM_LIMIT_BYTES),
        name="in_proj",
    )(x, ctx, mods3, mods3, norm1_g, w_in, k_norm_col, q_norm_col, gm_norm_col, cos_t, sin_t)


def _attention_kernel(qt_ref, k_ref, vt_ref, o_ref, qz_ref, m_ref, acc_ref, *s_refs):
    def one_kv_head(kv_head, carry):
        _attention_kv_head(kv_head, qt_ref, k_ref, vt_ref, o_ref, qz_ref, m_ref, acc_ref, s_refs)
        return carry

    lax.fori_loop(0, N_KV_HEADS, one_kv_head, 0)


def _attention_kv_head(kv_head, qt_ref, k_ref, vt_ref, o_ref, qz_ref, m_ref, acc_ref, s_refs):
    tq = qt_ref.shape[3]
    n_tiles = k_ref.shape[1] // KV_TILE

    q_cat = jnp.concatenate([qt_ref[0, Q_PER_KV * kv_head + g] for g in range(Q_PER_KV)], axis=1)
    zeros = jnp.zeros_like(q_cat)
    qz_ref[...] = jnp.concatenate(
        [jnp.where(kv_head == 0, q_cat, zeros), jnp.where(kv_head == 1, q_cat, zeros)], axis=0)

    def tile_slice(i):
        return pl.ds(i * KV_TILE if isinstance(i, int) else pl.multiple_of(i * KV_TILE, KV_TILE), KV_TILE)

    col_blocks = [slice(h * COL_BLOCK, (h + 1) * COL_BLOCK) for h in range(m_ref.shape[1] // COL_BLOCK)]

    def step(i_prod, buf_prod, i_cons, buf_cons, s_max):
        if i_prod is not None:
            k_t = k_ref[0, tile_slice(i_prod), :]
        if i_cons is not None:
            vt = vt_ref[0, kv_head, :, tile_slice(i_cons)]
            v1t = jnp.concatenate([vt, jnp.ones((ONES_ROWS, KV_TILE), vt.dtype)], axis=0)
        new_max = []
        for h, cols in enumerate(col_blocks):
            if i_prod is not None:
                s = jnp.dot(k_t, qz_ref[:, cols], preferred_element_type=F32)
                s_refs[buf_prod][:, cols] = s
                new_max.append(jnp.max(s, axis=0, keepdims=True))
            if i_cons is not None:
                m_old = m_ref[:, cols]
                m_new = jnp.maximum(m_old, s_max[h])
                m_ref[:, cols] = m_new
                p = jnp.exp2(s_refs[buf_cons][:, cols] - m_new).astype(BF16)
                acc_ref[:, cols] = jnp.exp2(m_old - m_new) * acc_ref[:, cols] + jnp.dot(
                    v1t, p, preferred_element_type=F32)
        return tuple(new_max)

    m_ref[...] = jnp.full(m_ref.shape, -jnp.inf, F32)
    acc_ref[...] = jnp.zeros(acc_ref.shape, F32)

    depth = PIPE_DEPTH

    def steps(first_tile, count, maxes):
        maxes = list(maxes)
        for j in range(count):
            new = step(first_tile + j + depth, (j + depth) % N_SCORE_BUFS, first_tile + j, j % N_SCORE_BUFS, maxes[0])
            maxes = maxes[1:] + [new]
        return tuple(maxes)

    assert STEPS_PER_TRIP % N_SCORE_BUFS == 0 and N_SCORE_BUFS > depth
    maxes = tuple(step(i, i, None, None, None) for i in range(depth))
    n_steps = n_tiles - depth
    n_trips = n_steps // STEPS_PER_TRIP
    maxes = lax.fori_loop(0, n_trips, lambda t, mx: steps(t * STEPS_PER_TRIP, STEPS_PER_TRIP, mx), maxes)
    done = n_trips * STEPS_PER_TRIP
    maxes = steps(done, n_steps - done, maxes)
    for j in range(depth):
        step(None, None, n_steps + j, (n_steps + j) % N_SCORE_BUFS, maxes[j])

    acc = acc_ref[...]
    o_t = acc[:HEAD_DIM] / acc[HEAD_DIM:HEAD_DIM + 1]
    pairs = []
    for g in range(0, Q_PER_KV, 2):
        two = jnp.concatenate([o_t[:, g * tq:(g + 1) * tq], o_t[:, (g + 1) * tq:(g + 2) * tq]], axis=0)
        pairs.append(two.T)
    width = Q_PER_KV * HEAD_DIM
    o_ref[0, :, pl.ds(pl.multiple_of(kv_head * width, width), width)] = (
        jnp.concatenate(pairs, axis=1).astype(o_ref.dtype))


def _attention(qt, k, vt):
    b, _, _, n = qt.shape
    n_keys = k.shape[1]
    assert n_keys % KV_TILE == 0
    tq = Q_TILE
    cols = Q_PER_KV * tq
    return pl.pallas_call(
        _attention_kernel,
        grid=(b, n // tq),
        in_specs=[
            pl.BlockSpec((1, N_Q_HEADS, HEAD_DIM, tq), lambda i, t: (i, 0, 0, t)),
            pl.BlockSpec((1, n_keys, K_W), lambda i, t: (i, 0, 0)),
            pl.BlockSpec((1, N_KV_HEADS, HEAD_DIM, n_keys), lambda i, t: (i, 0, 0, 0)),
        ],
        out_specs=pl.BlockSpec((1, tq, ATTN_WIDTH), lambda i, t: (i, t, 0)),
        out_shape=jax.ShapeDtypeStruct((b, n, ATTN_WIDTH), BF16),
        scratch_shapes=[
            pltpu.VMEM((K_W, cols), BF16),
            pltpu.VMEM((1, cols), F32),
            pltpu.VMEM((HEAD_DIM + ONES_ROWS, cols), F32),
        ] + [pltpu.VMEM((KV_TILE, cols), F32)] * N_SCORE_BUFS,
        compiler_params=pltpu.CompilerParams(
            dimension_semantics=("arbitrary", "arbitrary"),
            vmem_limit_bytes=V7X_VMEM_LIMIT_BYTES),
        name="attention",
    )(qt, k, vt)


def _post_kernel(x_ref, mod_ref, attn_ref, u_ref, vg_ref, ga_ref, gb_ref, ws_ref, bs_ref,
                 wa_ref, wg_ref, wo_ref, g2_ref, w1_ref, w2_ref, o_ref):
    tm = x_ref.shape[1]
    mod = mod_ref[0]
    gate1, shift2, scale2, gate2 = mod[2:3], mod[3:4], mod[4:5], mod[5:6]

    lane = lax.broadcasted_iota(jnp.int32, (CHUNK, 2 * GM_GROUP_DIM), 1)
    first_half = lane < GM_GROUP_DIM
    vg = vg_ref[0]
    chunks = []
    for c in range(tm // CHUNK):
        cols = []
        for gp in range(GM_GROUPS // 2):
            rhs = vg[c * CHUNK:(c + 1) * CHUNK, gp * 128:(gp + 1) * 128]
            s0 = jnp.dot(ws_ref[2 * gp], rhs, preferred_element_type=F32)
            s1 = jnp.dot(ws_ref[2 * gp + 1], rhs, preferred_element_type=F32)
            cols.append(jnp.where(first_half, s0, s1))
        chunks.append(jnp.concatenate(cols, axis=1) + bs_ref[...])
    s = jnp.concatenate(chunks, axis=0)
    gm = (u_ref[0].astype(F32) * s).astype(BF16)

    y = (_sigmoid(ga_ref[0].astype(F32)) * jnp.dot(attn_ref[0], wa_ref[...], preferred_element_type=F32)
         + _sigmoid(gb_ref[0].astype(F32)) * jnp.dot(gm, wg_ref[...], preferred_element_type=F32))
    x1 = x_ref[0] + gate1 * jnp.dot(y.astype(BF16), wo_ref[...], preferred_element_type=F32)

    h2 = _modulate(x1, g2_ref[...], shift2, scale2).astype(BF16)
    f = jnp.maximum(jnp.dot(h2, w1_ref[...], preferred_element_type=F32), 0.0)
    f = (f * f).astype(BF16)
    o_ref[0] = x1 + gate2 * jnp.dot(f, w2_ref[...], preferred_element_type=F32)


def _post(x, mods3, attn, u, vg, ga, gb, gm_ws, bs_full, w_br_attn, w_br_gm, w_out, norm2_g, w_ff1, w_ff2):
    b, n, d = x.shape
    tm = POST_TILE
    tok = lambda w: pl.BlockSpec((1, tm, w), lambda i, t: (i, t, 0))
    return pl.pallas_call(
        _post_kernel,
        grid=(b, n // tm),
        in_specs=[
            tok(d),
            pl.BlockSpec((1, N_MOD, d), lambda i, t: (i, 0, 0)),
            tok(ATTN_WIDTH), tok(GM_WIDTH), tok(GM_WIDTH), tok(d), tok(d),
            _const_spec((GM_GROUPS, CHUNK, CHUNK)),
            _const_spec((CHUNK, GM_WIDTH)),
            _const_spec((ATTN_WIDTH, d)),
            _const_spec((GM_WIDTH, d)),
            _const_spec((d, d)),
            _const_spec((1, d)),
            _const_spec((d, D_FF)),
            _const_spec((D_FF, d)),
        ],
        out_specs=tok(d),
        out_shape=jax.ShapeDtypeStruct((b, n, d), F32),
        compiler_params=pltpu.CompilerParams(
            dimension_semantics=("arbitrary", "arbitrary"), vmem_limit_bytes=V7X_VMEM_LIMIT_BYTES),
        name="post",
    )(x, mods3, attn, u, vg, ga, gb, gm_ws, bs_full, w_br_attn, w_br_gm, w_out, norm2_g, w_ff1, w_ff2)


def _rope_tables_t(n_tokens, n_ctx):
    rows = n_tokens // GRID_W
    row = jnp.repeat(jnp.arange(rows, dtype=F32), GRID_W)
    col = jnp.tile(jnp.arange(GRID_W, dtype=F32), rows)
    inv = ROPE_THETA ** (-jnp.arange(0, ROT_AXIS_DIM, 2, dtype=F32) / ROT_AXIS_DIM)
    ang_t = jnp.concatenate([inv[:, None] * row[None, :], inv[:, None] * col[None, :]], axis=0)
    cos_t = jnp.concatenate([jnp.ones((ROT_AXIS_DIM, n_ctx), F32), jnp.cos(ang_t)], axis=1)
    sin_t = jnp.concatenate([jnp.zeros((ROT_AXIS_DIM, n_ctx), F32), jnp.sin(ang_t)], axis=1)
    return cos_t, sin_t


def kernel(x, c, ctx, c_ctx, w_mod, b_mod, norm1_g, norm2_g, w_in, q_norm_g, k_norm_g, gm_norm_g,
           gm_ws, gm_bs, w_br_attn, w_br_gm, w_out, w_ff1, w_ff2):
    b, n, d = x.shape
    assert w_mod.shape[0] == 1, "single-layer block"
    assert (b, d) == c.shape and d == D_MODEL and n % max(TOK_TILE, Q_TILE) == 0

    pad_rows = (-(b + 1)) % 8
    c_all = jnp.concatenate([c, c_ctx[None, :], jnp.zeros((pad_rows, d), c.dtype)], axis=0)
    mods3 = _adaln(c_all, w_mod[0], b_mod[0]).reshape(b + 1 + pad_rows, N_MOD, d)

    w_in_b = w_in[0].astype(BF16)
    k_norm_col = k_norm_g[0].reshape(HEAD_DIM, 1)
    q_norm_col = q_norm_g[0].reshape(HEAD_DIM, 1)
    gm_norm_col = gm_norm_g[0].reshape(GM_WIDTH, 1)
    cos_t, sin_t = _rope_tables_t(n, ctx.shape[1])

    k, vt, qt, u, vg, ga, gb = _in_proj(
        x, ctx, mods3, norm1_g, w_in_b, k_norm_col, q_norm_col, gm_norm_col, cos_t, sin_t)
    attn = _attention(qt, k, vt)

    bs_full = jnp.repeat(gm_bs[0].T, GM_GROUP_DIM, axis=1)
    return _post(x, mods3, attn, u, vg, ga, gb, gm_ws[0].astype(BF16), bs_full,
                 w_br_attn[0].astype(BF16), w_br_gm[0].astype(BF16), w_out[0].astype(BF16),
                 norm2_g, w_ff1[0].astype(BF16), w_ff2[0].astype(BF16))
```

```python
import functools
import math

import jax
import jax.numpy as jnp
from jax import lax
from jax.experimental import pallas as pl
from jax.experimental.pallas import tpu as pltpu

F32 = jnp.float32
BF16 = jnp.bfloat16

D_MODEL = 1024
GRID_W = 64
HEAD_DIM = 64
N_Q_HEADS = 8
N_KV_HEADS = 2
Q_PER_KV = N_Q_HEADS // N_KV_HEADS
ATTN_WIDTH = N_Q_HEADS * HEAD_DIM
GM_GROUPS = 8
GM_GROUP_DIM = 64
GM_WIDTH = GM_GROUPS * GM_GROUP_DIM
CHUNK = 128
D_FF = 4 * D_MODEL
ROPE_THETA = 10000.0
ROT_AXIS_DIM = HEAD_DIM // 2
EPS = 1e-6
N_MOD = 6

K_W = N_KV_HEADS * HEAD_DIM
KV_COLS = 2 * K_W
Q_OFF = KV_COLS
U_OFF = Q_OFF + ATTN_WIDTH
VG_OFF = U_OFF + GM_WIDTH
GA_OFF = VG_OFF + GM_WIDTH
GB_OFF = GA_OFF + D_MODEL
D_IN = GB_OFF + D_MODEL

V7X_VMEM_LIMIT_BYTES = 56 * 1024 * 1024

TOK_TILE = 256
W_CAST_ROWS = 128
POST_TILE = 512
Q_TILE = 1024
KV_TILE = 256
PIPE_DEPTH = 1
N_SCORE_BUFS = 2
STEPS_PER_TRIP = 14
COL_BLOCK = 256
ONES_ROWS = 16

Q_SCALE_LOG2E = (HEAD_DIM ** -0.5) * math.log2(math.e)


def _sigmoid(x):
    return 1.0 / (1.0 + jnp.exp(-x))


def _gelu_tanh(x):
    c = math.sqrt(2.0 / math.pi)
    return 0.5 * x * (1.0 + jnp.tanh(c * (x + 0.044715 * (x * x * x))))


def _modulate(x, g, shift, scale):
    ms = jnp.mean(x * x, axis=-1, keepdims=True)
    return (x * lax.rsqrt(ms + EPS) * g) * (1.0 + scale) + shift


def _const_spec(shape):
    return pl.BlockSpec(shape, lambda *_: (0,) * len(shape), pipeline_mode=pl.Buffered(1))


def _adaln_kernel(c_ref, w_ref, b_ref, o_ref):
    c = c_ref[...]
    a = (c * _sigmoid(c)).astype(BF16)
    o_ref[...] = jnp.dot(a, w_ref[...].astype(BF16), preferred_element_type=F32) + b_ref[...]


def _adaln(c_all, w_mod, b_mod):
    rows, d = c_all.shape
    n = w_mod.shape[1]
    tn = 1024
    return pl.pallas_call(
        _adaln_kernel,
        grid=(n // tn,),
        in_specs=[
            pl.BlockSpec((rows, d), lambda j: (0, 0)),
            pl.BlockSpec((d, tn), lambda j: (0, j)),
            pl.BlockSpec((1, tn), lambda j: (0, j)),
        ],
        out_specs=pl.BlockSpec((rows, tn), lambda j: (0, j)),
        out_shape=jax.ShapeDtypeStruct((rows, n), F32),
        compiler_params=pltpu.CompilerParams(
            dimension_semantics=("arbitrary",), vmem_limit_bytes=V7X_VMEM_LIMIT_BYTES),
        name="adaln",
    )(c_all, w_mod, b_mod.reshape(1, n))


def _head_norm_t(blk, g_col):
    ms = jnp.mean(blk * blk, axis=0, keepdims=True)
    return blk * lax.rsqrt(ms + EPS) * g_col


def _rope_t(y, cos_t, sin_t):
    y1, y2 = y[:ROT_AXIS_DIM], y[ROT_AXIS_DIM:]
    return jnp.concatenate([y1 * cos_t - y2 * sin_t, y2 * cos_t + y1 * sin_t], axis=0)


def _in_proj_kernel(x_next_ref, ctx_ref, mod_ctx_ref, mod_ref, g1_ref, w_ref, kg_ref, qg_ref, gmg_ref,
                    cos_ref, sin_ref, k_ref, vt_ref, qt_ref, u_ref, vg_ref, ga_ref, gb_ref, h_ref, wb_ref):
    @pl.when((pl.program_id(0) == 0) & (pl.program_id(1) == 0))
    def _():
        for r in range(0, wb_ref.shape[0], W_CAST_ROWS):
            wb_ref[r:r + W_CAST_ROWS] = w_ref[0, r:r + W_CAST_ROWS].astype(BF16)

    @pl.when(pl.program_id(1) == 0)
    def _():
        mod_ctx = mod_ctx_ref[0]
        h_ref[...] = _modulate(ctx_ref[0], g1_ref[...], mod_ctx[0:1], mod_ctx[1:2]).astype(BF16)

    p = jnp.dot(h_ref[...], wb_ref[...], preferred_element_type=F32)

    cos_t = cos_ref[...]
    sin_t = sin_ref[...]
    pt = p[:, :U_OFF].T
    kg = kg_ref[...]
    kt = jnp.concatenate(
        [_rope_t(_head_norm_t(pt[HEAD_DIM * j:HEAD_DIM * (j + 1)], kg), cos_t, sin_t)
         for j in range(N_KV_HEADS)], axis=0)
    k_ref[0] = kt.T.astype(BF16)
    for j in range(N_KV_HEADS):
        vt_ref[0, j] = pt[K_W + HEAD_DIM * j:K_W + HEAD_DIM * (j + 1)].astype(BF16)
    qg = qg_ref[...]
    for j in range(N_Q_HEADS):
        blk = pt[Q_OFF + HEAD_DIM * j:Q_OFF + HEAD_DIM * (j + 1)]
        qt_ref[0, j] = (_rope_t(_head_norm_t(blk, qg), cos_t, sin_t) * Q_SCALE_LOG2E).astype(BF16)

    u_ref[0] = _gelu_tanh(p[:, U_OFF:VG_OFF]).astype(BF16)
    vgt = _gelu_tanh(p[:, VG_OFF:GA_OFF]).T
    gmg = gmg_ref[...]
    vgn_t = jnp.concatenate(
        [_head_norm_t(vgt[GM_GROUP_DIM * g:GM_GROUP_DIM * (g + 1)], gmg[GM_GROUP_DIM * g:GM_GROUP_DIM * (g + 1)])
         for g in range(GM_GROUPS)], axis=0)
    vg_ref[0] = vgn_t.T.astype(BF16)
    ga_ref[0] = p[:, GA_OFF:GB_OFF].astype(BF16)
    gb_ref[0] = p[:, GB_OFF:D_IN].astype(BF16)

    mod = mod_ref[0]
    h_ref[...] = _modulate(x_next_ref[0], g1_ref[...], mod[0:1], mod[1:2]).astype(BF16)


def _in_proj(x, ctx, mods3, norm1_g, w_in, k_norm_col, q_norm_col, gm_norm_col, cos_t, sin_t):
    b, n, d = x.shape
    tm = TOK_TILE
    assert ctx.shape == (b, tm, d)
    n_keys = n + tm
    lat = lambda t: jnp.maximum(t - 1, 0)
    tok = lambda w: pl.BlockSpec((1, tm, w), lambda i, t: (i, lat(t), 0))
    return pl.pallas_call(
        _in_proj_kernel,
        grid=(b, n // tm + 1),
        in_specs=[
            pl.BlockSpec((1, tm, d), lambda i, t: (i, jnp.minimum(t, n // tm - 1), 0)),
            pl.BlockSpec((1, tm, d), lambda i, t: (i, 0, 0)),
            pl.BlockSpec((1, N_MOD, d), lambda i, t: (b, 0, 0)),
            pl.BlockSpec((1, N_MOD, d), lambda i, t: (i, 0, 0)),
            _const_spec((1, d)),
            _const_spec((1, d, D_IN)),
            _const_spec((HEAD_DIM, 1)),
            _const_spec((HEAD_DIM, 1)),
            _const_spec((GM_WIDTH, 1)),
            pl.BlockSpec((ROT_AXIS_DIM, tm), lambda i, t: (0, t)),
            pl.BlockSpec((ROT_AXIS_DIM, tm), lambda i, t: (0, t)),
        ],
        out_specs=[
            pl.BlockSpec((1, tm, K_W), lambda i, t: (i, t, 0)),
            pl.BlockSpec((1, N_KV_HEADS, HEAD_DIM, tm), lambda i, t: (i, 0, 0, t)),
            pl.BlockSpec((1, N_Q_HEADS, HEAD_DIM, tm), lambda i, t: (i, 0, 0, lat(t))),
            tok(GM_WIDTH), tok(GM_WIDTH), tok(d), tok(d),
        ],
        out_shape=[
            jax.ShapeDtypeStruct((b, n_keys, K_W), BF16),
            jax.ShapeDtypeStruct((b, N_KV_HEADS, HEAD_DIM, n_keys), BF16),
            jax.ShapeDtypeStruct((b, N_Q_HEADS, HEAD_DIM, n), BF16),
            jax.ShapeDtypeStruct((b, n, GM_WIDTH), BF16),
            jax.ShapeDtypeStruct((b, n, GM_WIDTH), BF16),
            jax.ShapeDtypeStruct((b, n, d), BF16),
            jax.ShapeDtypeStruct((b, n, d), BF16),
        ],
        scratch_shapes=[pltpu.VMEM((tm, d), BF16), pltpu.VMEM((d, D_IN), BF16)],
        compiler_params=pltpu.CompilerParams(
            dimension_semantics=("arbitrary", "arbitrary"), vmem_limit_bytes=V7X_VMEM_LIMIT_BYTES),
        name="in_proj",
    )(x, ctx, mods3, mods3, norm1_g, w_in, k_norm_col, q_norm_col, gm_norm_col, cos_t, sin_t)


def _attention_kernel(qt_ref, k_ref, vt_ref, o_ref, qz_ref, m_ref, acc_ref, *s_refs):
    def one_kv_head(kv_head, carry):
        _attention_kv_head(kv_head, qt_ref, k_ref, vt_ref, o_ref, qz_ref, m_ref, acc_ref, s_refs)
        return carry

    lax.fori_loop(0, N_KV_HEADS, one_kv_head, 0)


def _attention_kv_head(kv_head, qt_ref, k_ref, vt_ref, o_ref, qz_ref, m_ref, acc_ref, s_refs):
    tq = qt_ref.shape[3]
    n_tiles = k_ref.shape[1] // KV_TILE

    q_cat = jnp.concatenate([qt_ref[0, Q_PER_KV * kv_head + g] for g in range(Q_PER_KV)], axis=1)
    zeros = jnp.zeros_like(q_cat)
    qz_ref[...] = jnp.concatenate(
        [jnp.where(kv_head == 0, q_cat, zeros), jnp.where(kv_head == 1, q_cat, zeros)], axis=0)

    def tile_slice(i):
        return pl.ds(i * KV_TILE if isinstance(i, int) else pl.multiple_of(i * KV_TILE, KV_TILE), KV_TILE)

    col_blocks = [slice(h * COL_BLOCK, (h + 1) * COL_BLOCK) for h in range(m_ref.shape[1] // COL_BLOCK)]

    def step(i_prod, buf_prod, i_cons, buf_cons, s_max):
        if i_prod is not None:
            k_t = k_ref[0, tile_slice(i_prod), :]
        if i_cons is not None:
            vt = vt_ref[0, kv_head, :, tile_slice(i_cons)]
            v1t = jnp.concatenate([vt, jnp.ones((ONES_ROWS, KV_TILE), vt.dtype)], axis=0)
        new_max = []
        for h, cols in enumerate(col_blocks):
            if i_prod is not None:
                s = jnp.dot(k_t, qz_ref[:, cols], preferred_element_type=F32)
                s_refs[buf_prod][:, cols] = s
                new_max.append(jnp.max(s, axis=0, keepdims=True))
            if i_cons is not None:
                m_old = m_ref[:, cols]
                m_new = jnp.maximum(m_old, s_max[h])
                m_ref[:, cols] = m_new
                p = jnp.exp2(s_refs[buf_cons][:, cols] - m_new).astype(BF16)
                acc_ref[:, cols] = jnp.exp2(m_old - m_new) * acc_ref[:, cols] + jnp.dot(
                    v1t, p, preferred_element_type=F32)
        return tuple(new_max)

    m_ref[...] = jnp.full(m_ref.shape, -jnp.inf, F32)
    acc_ref[...] = jnp.zeros(acc_ref.shape, F32)

    depth = PIPE_DEPTH

    def steps(first_tile, count, maxes):
        maxes = list(maxes)
        for j in range(count):
            new = step(first_tile + j + depth, (j + depth) % N_SCORE_BUFS, first_tile + j, j % N_SCORE_BUFS, maxes[0])
            maxes = maxes[1:] + [new]
        return tuple(maxes)

    assert STEPS_PER_TRIP % N_SCORE_BUFS == 0 and N_SCORE_BUFS > depth
    maxes = tuple(step(i, i, None, None, None) for i in range(depth))
    n_steps = n_tiles - depth
    n_trips = n_steps // STEPS_PER_TRIP
    maxes = lax.fori_loop(0, n_trips, lambda t, mx: steps(t * STEPS_PER_TRIP, STEPS_PER_TRIP, mx), maxes)
    done = n_trips * STEPS_PER_TRIP
    maxes = steps(done, n_steps - done, maxes)
    for j in range(depth):
        step(None, None, n_steps + j, (n_steps + j) % N_SCORE_BUFS, maxes[j])

    acc = acc_ref[...]
    o_t = acc[:HEAD_DIM] / acc[HEAD_DIM:HEAD_DIM + 1]
    pairs = []
    for g in range(0, Q_PER_KV, 2):
        two = jnp.concatenate([o_t[:, g * tq:(g + 1) * tq], o_t[:, (g + 1) * tq:(g + 2) * tq]], axis=0)
        pairs.append(two.T)
    width = Q_PER_KV * HEAD_DIM
    o_ref[0, :, pl.ds(pl.multiple_of(kv_head * width, width), width)] = (
        jnp.concatenate(pairs, axis=1).astype(o_ref.dtype))


def _attention(qt, k, vt):
    b, _, _, n = qt.shape
    n_keys = k.shape[1]
    assert n_keys % KV_TILE == 0
    tq = Q_TILE
    cols = Q_PER_KV * tq
    return pl.pallas_call(
        _attention_kernel,
        grid=(b, n // tq),
        in_specs=[
            pl.BlockSpec((1, N_Q_HEADS, HEAD_DIM, tq), lambda i, t: (i, 0, 0, t)),
            pl.BlockSpec((1, n_keys, K_W), lambda i, t: (i, 0, 0)),
            pl.BlockSpec((1, N_KV_HEADS, HEAD_DIM, n_keys), lambda i, t: (i, 0, 0, 0)),
        ],
        out_specs=pl.BlockSpec((1, tq, ATTN_WIDTH), lambda i, t: (i, t, 0)),
        out_shape=jax.ShapeDtypeStruct((b, n, ATTN_WIDTH), BF16),
        scratch_shapes=[
            pltpu.VMEM((K_W, cols), BF16),
            pltpu.VMEM((1, cols), F32),
            pltpu.VMEM((HEAD_DIM + ONES_ROWS, cols), F32),
        ] + [pltpu.VMEM((KV_TILE, cols), F32)] * N_SCORE_BUFS,
        compiler_params=pltpu.CompilerParams(
            dimension_semantics=("arbitrary", "arbitrary"),
            vmem_limit_bytes=V7X_VMEM_LIMIT_BYTES),
        name="attention",
    )(qt, k, vt)


def _post_kernel(x_ref, mod_ref, attn_ref, u_ref, vg_ref, ga_ref, gb_ref, ws_ref, bs_ref,
                 wa_ref, wg_ref, wo_ref, g2_ref, w1_ref, w2_ref, o_ref):
    tm = x_ref.shape[1]
    mod = mod_ref[0]
    gate1, shift2, scale2, gate2 = mod[2:3], mod[3:4], mod[4:5], mod[5:6]

    lane = lax.broadcasted_iota(jnp.int32, (CHUNK, 2 * GM_GROUP_DIM), 1)
    first_half = lane < GM_GROUP_DIM
    vg = vg_ref[0]
    chunks = []
    for c in range(tm // CHUNK):
        cols = []
        for gp in range(GM_GROUPS // 2):
            rhs = vg[c * CHUNK:(c + 1) * CHUNK, gp * 128:(gp + 1) * 128]
            s0 = jnp.dot(ws_ref[2 * gp], rhs, preferred_element_type=F32)
            s1 = jnp.dot(ws_ref[2 * gp + 1], rhs, preferred_element_type=F32)
            cols.append(jnp.where(first_half, s0, s1))
        chunks.append(jnp.concatenate(cols, axis=1) + bs_ref[...])
    s = jnp.concatenate(chunks, axis=0)
    gm = (u_ref[0].astype(F32) * s).astype(BF16)

    y = (_sigmoid(ga_ref[0].astype(F32)) * jnp.dot(attn_ref[0], wa_ref[...], preferred_element_type=F32)
         + _sigmoid(gb_ref[0].astype(F32)) * jnp.dot(gm, wg_ref[...], preferred_element_type=F32))
    x1 = x_ref[0] + gate1 * jnp.dot(y.astype(BF16), wo_ref[...], preferred_element_type=F32)

    h2 = _modulate(x1, g2_ref[...], shift2, scale2).astype(BF16)
    f = jnp.maximum(jnp.dot(h2, w1_ref[...], preferred_element_type=F32), 0.0)
    f = (f * f).astype(BF16)
    o_ref[0] = x1 + gate2 * jnp.dot(f, w2_ref[...], preferred_element_type=F32)


def _post(x, mods3, attn, u, vg, ga, gb, gm_ws, bs_full, w_br_attn, w_br_gm, w_out, norm2_g, w_ff1, w_ff2):
    b, n, d = x.shape
    tm = POST_TILE
    tok = lambda w: pl.BlockSpec((1, tm, w), lambda i, t: (i, t, 0))
    return pl.pallas_call(
        _post_kernel,
        grid=(b, n // tm),
        in_specs=[
            tok(d),
            pl.BlockSpec((1, N_MOD, d), lambda i, t: (i, 0, 0)),
            tok(ATTN_WIDTH), tok(GM_WIDTH), tok(GM_WIDTH), tok(d), tok(d),
            _const_spec((GM_GROUPS, CHUNK, CHUNK)),
            _const_spec((CHUNK, GM_WIDTH)),
            _const_spec((ATTN_WIDTH, d)),
            _const_spec((GM_WIDTH, d)),
            _const_spec((d, d)),
            _const_spec((1, d)),
            _const_spec((d, D_FF)),
            _const_spec((D_FF, d)),
        ],
        out_specs=tok(d),
        out_shape=jax.ShapeDtypeStruct((b, n, d), F32),
        compiler_params=pltpu.CompilerParams(
            dimension_semantics=("arbitrary", "arbitrary"), vmem_limit_bytes=V7X_VMEM_LIMIT_BYTES),
        name="post",
    )(x, mods3, attn, u, vg, ga, gb, gm_ws, bs_full, w_br_attn, w_br_gm, w_out, norm2_g, w_ff1, w_ff2)


def _rope_tables_t(n_tokens, n_ctx):
    rows = n_tokens // GRID_W
    row = jnp.repeat(jnp.arange(rows, dtype=F32), GRID_W)
    col = jnp.tile(jnp.arange(GRID_W, dtype=F32), rows)
    inv = ROPE_THETA ** (-jnp.arange(0, ROT_AXIS_DIM, 2, dtype=F32) / ROT_AXIS_DIM)
    ang_t = jnp.concatenate([inv[:, None] * row[None, :], inv[:, None] * col[None, :]], axis=0)
    cos_t = jnp.concatenate([jnp.ones((ROT_AXIS_DIM, n_ctx), F32), jnp.cos(ang_t)], axis=1)
    sin_t = jnp.concatenate([jnp.zeros((ROT_AXIS_DIM, n_ctx), F32), jnp.sin(ang_t)], axis=1)
    return cos_t, sin_t


def kernel(x, c, ctx, c_ctx, w_mod, b_mod, norm1_g, norm2_g, w_in, q_norm_g, k_norm_g, gm_norm_g,
           gm_ws, gm_bs, w_br_attn, w_br_gm, w_out, w_ff1, w_ff2):
    b, n, d = x.shape
    assert w_mod.shape[0] == 1, "single-layer block"
    assert (b, d) == c.shape and d == D_MODEL and n % max(TOK_TILE, Q_TILE) == 0

    pad_rows = (-(b + 1)) % 8
    c_all = jnp.concatenate([c, c_ctx[None, :], jnp.zeros((pad_rows, d), c.dtype)], axis=0)
    mods3 = _adaln(c_all, w_mod[0], b_mod[0]).reshape(b + 1 + pad_rows, N_MOD, d)

    k_norm_col = k_norm_g[0].reshape(HEAD_DIM, 1)
    q_norm_col = q_norm_g[0].reshape(HEAD_DIM, 1)
    gm_norm_col = gm_norm_g[0].reshape(GM_WIDTH, 1)
    cos_t, sin_t = _rope_tables_t(n, ctx.shape[1])

    k, vt, qt, u, vg, ga, gb = _in_proj(
        x, ctx, mods3, norm1_g, w_in, k_norm_col, q_norm_col, gm_norm_col, cos_t, sin_t)
    attn = _attention(qt, k, vt)

    bs_full = jnp.repeat(gm_bs[0].T, GM_GROUP_DIM, axis=1)
    return _post(x, mods3, attn, u, vg, ga, gb, gm_ws[0].astype(BF16), bs_full,
                 w_br_attn[0].astype(BF16), w_br_gm[0].astype(BF16), w_out[0].astype(BF16),
                 norm2_g, w_ff1[0].astype(BF16), w_ff2[0].astype(BF16))
```
